```python
import jax, jax.numpy as jnp
from jax import lax
import numpy as np

D_MODEL = 1024
BATCH = 32
SEQ = 2048
DEPTH = 4

N_MIXERS = 2
N_HEADS = 16
N_KV_HEADS = 2
HEAD_DIM = 64
GROUP = N_HEADS // N_KV_HEADS
WINDOW = 128
BLOCK = 128
SPAN = BLOCK + WINDOW
Q_DIM = N_HEADS * HEAD_DIM
KV_DIM = N_KV_HEADS * HEAD_DIM
QKV_DIM = Q_DIM + 2 * KV_DIM
CONV_WIDTH = 31
CONV_DIM = D_MODEL
D_FF = 4 * D_MODEL
N_MOD = 6
EPS = 1e-6
N_ATTN_LAYERS = (DEPTH + 1) // 2
N_CONV_LAYERS = DEPTH // 2

kernel_name = "interleaved_swa_sink_conformer_conv_adaln"


def alibi_slopes(n_heads):
    return jnp.asarray(np.array([2.0 ** (-8.0 * (h + 1) / n_heads) for h in range(n_heads)], dtype=np.float32))


def rmsnorm(x, g):
    xf = x.astype(jnp.float32)
    r = lax.rsqrt(jnp.mean(xf * xf, axis=-1, keepdims=True) + EPS)
    return (xf * r * g.astype(jnp.float32)).astype(x.dtype)


def layernorm(x, g, b):
    xf = x.astype(jnp.float32)
    mu = jnp.mean(xf, axis=-1, keepdims=True)
    var = jnp.mean(jnp.square(xf - mu), axis=-1, keepdims=True)
    y = (xf - mu) * lax.rsqrt(var + EPS) * g.astype(jnp.float32) + b.astype(jnp.float32)
    return y.astype(x.dtype)


def modulate(h, shift, scale):
    return h * (1 + scale[:, None, :]) + shift[:, None, :]


def sliding_window_attention(h, w_qkv, b_qkv, w_o, b_o, sinks):
    B, S, _ = h.shape
    qkv = h @ w_qkv + b_qkv
    q, k, v = jnp.split(qkv, [Q_DIM, Q_DIM + KV_DIM], axis=-1)
    q = q.reshape(B, S, N_KV_HEADS, GROUP, HEAD_DIM) * (HEAD_DIM ** -0.5)
    k = k.reshape(B, S, N_KV_HEADS, HEAD_DIM)
    v = v.reshape(B, S, N_KV_HEADS, HEAD_DIM)
    pad = ((0, 0), (WINDOW, 0), (0, 0), (0, 0))
    k_pad = jnp.pad(k, pad)
    v_pad = jnp.pad(v, pad)
    q_idx = jnp.arange(BLOCK)[:, None] + WINDOW
    k_idx = jnp.arange(SPAN)[None, :]
    dist = q_idx - k_idx
    band = (dist >= 0) & (dist < WINDOW)
    slopes = alibi_slopes(N_HEADS).reshape(N_KV_HEADS, GROUP)
    alibi = -slopes[:, :, None, None] * dist.astype(jnp.float32)[None, None]
    sink = sinks.astype(jnp.float32).reshape(N_KV_HEADS, GROUP)[None, :, :, None, None]

    def one_block(i):
        start = i * BLOCK
        qb = lax.dynamic_slice_in_dim(q, start, BLOCK, axis=1)
        kb = lax.dynamic_slice_in_dim(k_pad, start, SPAN, axis=1)
        vb = lax.dynamic_slice_in_dim(v_pad, start, SPAN, axis=1)
        s = jnp.einsum('bqkgd,bskd->bkgqs', qb, kb).astype(jnp.float32) + alibi
        mask = band & ((start - WINDOW + k_idx) >= 0)
        s = jnp.where(mask[None, None, None], s, -jnp.inf)
        m = jnp.maximum(jnp.max(s, axis=-1, keepdims=True), sink)
        p = jnp.exp(s - m)
        denom = jnp.sum(p, axis=-1, keepdims=True) + jnp.exp(sink - m)
        p = (p / denom).astype(vb.dtype)
        return jnp.einsum('bkgqs,bskd->bqkgd', p, vb)

    out = lax.map(one_block, jnp.arange(S // BLOCK))
    out = jnp.moveaxis(out, 0, 1).reshape(B, S, Q_DIM)
    return out @ w_o + b_o


def conformer_conv(h, w_pw1, b_pw1, w_dw, b_dw, ln_g, ln_b, w_pw2, b_pw2):
    u = h @ w_pw1 + b_pw1
    a, g = jnp.split(u, 2, axis=-1)
    u = a * jax.nn.sigmoid(g)
    u = jnp.pad(u, ((0, 0), (CONV_WIDTH - 1, 0), (0, 0)))
    u = lax.conv_general_dilated(u, w_dw[:, None, :].astype(u.dtype), window_strides=(1,), padding='VALID',
                                 dimension_numbers=('NWC', 'WIO', 'NWC'),
                                 feature_group_count=CONV_DIM) + b_dw
    u = jax.nn.silu(layernorm(u, ln_g, ln_b))
    return u @ w_pw2 + b_pw2


def squared_relu_mlp(h, w_up, w_down):
    return jnp.square(jax.nn.relu(h @ w_up)) @ w_down


def setup_inputs(seed: int = 0) -> dict:
    key = jax.random.key(seed)
    ks = jax.random.split(key, 32)
    D = D_MODEL
    nrm = lambda k, shape, s: jax.random.normal(k, shape, jnp.float32) * s
    return {
        "x": nrm(ks[0], (BATCH, SEQ, D), 1.0),
        "c": nrm(ks[1], (BATCH, D), 1.0),
        "w_mod": nrm(ks[2], (DEPTH, D, N_MOD * D), 0.5 * D ** -0.5),
        "b_mod": nrm(ks[3], (DEPTH, N_MOD * D), 0.01),
        "norm_mix": 1.0 + nrm(ks[4], (DEPTH, D), 0.02),
        "norm_mlp": 1.0 + nrm(ks[5], (DEPTH, D), 0.02),
        "w_qkv": nrm(ks[6], (N_ATTN_LAYERS, D, QKV_DIM), D ** -0.5),
        "b_qkv": nrm(ks[7], (N_ATTN_LAYERS, QKV_DIM), 0.01),
        "w_o": nrm(ks[8], (N_ATTN_LAYERS, Q_DIM, D), Q_DIM ** -0.5),
        "b_o": nrm(ks[9], (N_ATTN_LAYERS, D), 0.01),
        "sinks": nrm(ks[10], (N_ATTN_LAYERS, N_HEADS), 0.5),
        "w_pw1": nrm(ks[11], (N_CONV_LAYERS, D, 2 * CONV_DIM), D ** -0.5),
        "b_pw1": nrm(ks[12], (N_CONV_LAYERS, 2 * CONV_DIM), 0.01),
        "w_dw": nrm(ks[13], (N_CONV_LAYERS, CONV_WIDTH, CONV_DIM), CONV_WIDTH ** -0.5),
        "b_dw": nrm(ks[14], (N_CONV_LAYERS, CONV_DIM), 0.01),
        "conv_ln_g": 1.0 + nrm(ks[15], (N_CONV_LAYERS, CONV_DIM), 0.02),
        "conv_ln_b": nrm(ks[16], (N_CONV_LAYERS, CONV_DIM), 0.01),
        "w_pw2": nrm(ks[17], (N_CONV_LAYERS, CONV_DIM, D), CONV_DIM ** -0.5),
        "b_pw2": nrm(ks[18], (N_CONV_LAYERS, D), 0.01),
        "w_up": nrm(ks[19], (DEPTH, D, D_FF), D ** -0.5),
        "w_down": nrm(ks[20], (DEPTH, D_FF, D), D_FF ** -0.5),
        "final_norm": 1.0 + nrm(ks[21], (D,), 0.02),
    }


def reference(x, c, w_mod, b_mod, norm_mix, norm_mlp, w_qkv, b_qkv, w_o, b_o, sinks,
              w_pw1, b_pw1, w_dw, b_dw, conv_ln_g, conv_ln_b, w_pw2, b_pw2,
              w_up, w_down, final_norm):
    cs = jax.nn.silu(c)
    for i in range(DEPTH):
        mod = cs @ w_mod[i] + b_mod[i]
        sh1, sc1, g1, sh2, sc2, g2 = jnp.split(mod, N_MOD, axis=-1)
        h = modulate(rmsnorm(x, norm_mix[i]), sh1, sc1)
        j = i // N_MIXERS
        if i % N_MIXERS == 0:
            y = sliding_window_attention(h, w_qkv[j], b_qkv[j], w_o[j], b_o[j], sinks[j])
        else:
            y = conformer_conv(h, w_pw1[j], b_pw1[j], w_dw[j], b_dw[j], conv_ln_g[j], conv_ln_b[j],
                               w_pw2[j], b_pw2[j])
        x = x + g1[:, None, :] * y
        h = modulate(rmsnorm(x, norm_mlp[i]), sh2, sc2)
        x = x + g2[:, None, :] * squared_relu_mlp(h, w_up[i], w_down[i])
    return rmsnorm(x, final_norm)
```

```python
import functools

import jax
import jax.numpy as jnp
from jax import lax
from jax.experimental import pallas as pl
from jax.experimental.pallas import tpu as pltpu

D_MODEL = 1024
DEPTH = 4
N_HEADS = 16
N_KV_HEADS = 2
HEAD_DIM = 64
WINDOW = 128
BLOCK = 128
SPAN = BLOCK + WINDOW
Q_DIM = N_HEADS * HEAD_DIM
KV_DIM = N_KV_HEADS * HEAD_DIM
CONV_WIDTH = 31
D_FF = 4 * D_MODEL
N_MOD = 6
EPS = 1e-6

LANES = 128
CONV_HALO = 32
VMEM_LIMIT_BYTES = 56 * 1024 * 1024

MLP_TOKENS = 512
MLP_FF_CHUNK = 1024
ATTN_TOKENS = 512
CONV_TOKENS = 256
CONV_ROWS = 32
CONV_COLS = 256

F32 = jnp.float32
BF16 = jnp.bfloat16


def _modnorm(x, norm_w, scale, shift):
    r = lax.rsqrt(jnp.mean(x * x, axis=-1, keepdims=True) + EPS)
    return ((x * r) * (norm_w * (1.0 + scale)) + shift).astype(BF16)


def _mod_kernel(c_ref, w_ref, b_ref, o_ref):
    c = c_ref[...]
    cs = c * jax.nn.sigmoid(c)
    o_ref[0] = jnp.dot(cs, w_ref[0], preferred_element_type=F32) + b_ref[0]


def _modulation(c, w_mod, b_mod):
    depth, d, n = w_mod.shape
    batch = c.shape[0]
    nb = D_MODEL
    return pl.pallas_call(
        _mod_kernel,
        grid=(depth, n // nb),
        in_specs=[
            pl.BlockSpec((batch, d), lambda i, j: (0, 0)),
            pl.BlockSpec((1, d, nb), lambda i, j: (i, 0, j)),
            pl.BlockSpec((1, 1, nb), lambda i, j: (i, 0, j)),
        ],
        out_specs=pl.BlockSpec((1, batch, nb), lambda i, j: (i, 0, j)),
        out_shape=jax.ShapeDtypeStruct((depth, batch, n), F32),
        compiler_params=pltpu.CompilerParams(
            dimension_semantics=("arbitrary", "arbitrary"),
            vmem_limit_bytes=VMEM_LIMIT_BYTES),
        name="adaln_mod",
    )(c, w_mod, b_mod.reshape(depth, 1, n))


def _mlp_kernel(x_ref, sh_ref, sc_ref, g_ref, nw_ref, wup_ref, wdn_ref, fn_ref, o_ref, *, final):
    x = x_ref[0]
    h = _modnorm(x, nw_ref[...], sc_ref[0], sh_ref[0])
    acc = jnp.zeros(x.shape, F32)
    for j in range(D_FF // MLP_FF_CHUNK):
        cols = slice(j * MLP_FF_CHUNK, (j + 1) * MLP_FF_CHUNK)
        u = jnp.dot(h, wup_ref[:, cols], preferred_element_type=F32)
        u = jnp.square(jnp.maximum(u, 0.0)).astype(BF16)
        acc = acc + jnp.dot(u, wdn_ref[cols, :], preferred_element_type=F32)
    y = x + g_ref[0] * acc
    if final:
        r = lax.rsqrt(jnp.mean(y * y, axis=-1, keepdims=True) + EPS)
        y = y * r * fn_ref[...]
    o_ref[0] = y


def _mlp_layer(x, sh, sc, g, norm_w, w_up, w_dn, final_w, *, final):
    batch, seq, d = x.shape
    tm = MLP_TOKENS
    vec = pl.BlockSpec((1, 1, d), lambda b, t: (b, 0, 0))
    row = pl.BlockSpec((1, d), lambda b, t: (0, 0))
    tile = pl.BlockSpec((1, tm, d), lambda b, t: (b, t, 0))
    return pl.pallas_call(
        functools.partial(_mlp_kernel, final=final),
        grid=(batch, seq // tm),
        in_specs=[
            tile, vec, vec, vec, row,
            pl.BlockSpec((d, D_FF), lambda b, t: (0, 0), pipeline_mode=pl.Buffered(1)),
            pl.BlockSpec((D_FF, d), lambda b, t: (0, 0), pipeline_mode=pl.Buffered(1)),
            row,
        ],
        out_specs=tile,
        out_shape=jax.ShapeDtypeStruct(x.shape, F32),
        compiler_params=pltpu.CompilerParams(
            dimension_semantics=("arbitrary", "arbitrary"),
            vmem_limit_bytes=VMEM_LIMIT_BYTES),
        name="mlp_final" if final else "mlp",
    )(x, sh, sc, g, norm_w, w_up, w_dn, final_w)


def _alibi_slope(head):
    return 2.0 ** (-8.0 * (head + 1) / N_HEADS)


def _attn_kernel(sinks_ref, x_ref, sh_ref, sc_ref, g_ref, nw_ref, wqkv_ref, bqkv_ref,
                 wo_ref, bo_ref, o_ref, q_ref, kv_ref, bias_ref, *, ts):
    b = pl.program_id(0)
    t = pl.program_id(1)

    @pl.when((b == 0) & (t == 0))
    def _():
        qi = lax.broadcasted_iota(jnp.int32, (BLOCK, SPAN), 0) + WINDOW
        ki = lax.broadcasted_iota(jnp.int32, (BLOCK, SPAN), 1)
        dist = qi - ki
        band = (dist >= 0) & (dist < WINDOW)
        distf = dist.astype(F32)
        for hd in range(N_HEADS):
            bias = jnp.where(band, -_alibi_slope(hd) * distf, -jnp.inf)
            bias_ref[0, hd] = bias
            bias_ref[1, hd] = jnp.where(ki >= WINDOW, bias, -jnp.inf)

    @pl.when(t == 0)
    def _():
        kv_ref[0:WINDOW, :] = jnp.zeros((WINDOW, kv_ref.shape[1]), BF16)

    x = x_ref[0]
    h = _modnorm(x, nw_ref[...], sc_ref[0], sh_ref[0])
    qkv = jnp.dot(h, wqkv_ref[...], preferred_element_type=F32) + bqkv_ref[...]
    q_ref[...] = (qkv[:, :Q_DIM] * (HEAD_DIM ** -0.5)).astype(BF16)

    k = qkv[:, Q_DIM:Q_DIM + KV_DIM]
    v = qkv[:, Q_DIM + KV_DIM:]
    low = lax.broadcasted_iota(jnp.int32, (ts, LANES), 1) < HEAD_DIM
    copies = []
    for a in (k, v):
        swapped = pltpu.roll(a, HEAD_DIM, 1)
        copies += [jnp.where(low, a, 0.0), jnp.where(low, 0.0, swapped),
                   jnp.where(low, swapped, 0.0), jnp.where(low, 0.0, a)]
    for j, a in enumerate(copies):
        kv_ref[WINDOW:WINDOW + ts, j * LANES:(j + 1) * LANES] = a.astype(BF16)

    def block(qb, carry):
        r0 = pl.multiple_of(qb * BLOCK, BLOCK)
        first = jnp.where((t == 0) & (qb == 0), 1, 0)
        outs = []
        for pair in range(N_HEADS // 2):
            kvh = pair // (N_HEADS // 2 // N_KV_HEADS)
            qp = q_ref[pl.ds(r0, BLOCK), pair * LANES:(pair + 1) * LANES]
            o = None
            for half in range(2):
                hd = 2 * pair + half
                kcol = (2 * kvh + half) * LANES
                vcol = (2 * N_KV_HEADS + 2 * kvh + half) * LANES
                kk = kv_ref[pl.ds(r0, SPAN), kcol:kcol + LANES]
                vv = kv_ref[pl.ds(r0, SPAN), vcol:vcol + LANES]
                s = lax.dot_general(qp, kk, (((1,), (1,)), ((), ())),
                                    preferred_element_type=F32)
                s = s + bias_ref[first, hd]
                sink = sinks_ref[hd]
                m = jnp.maximum(jnp.max(s, axis=-1, keepdims=True), sink)
                p = jnp.exp(s - m)
                denom = jnp.sum(p, axis=-1, keepdims=True) + jnp.exp(sink - m)
                p = (p * (1.0 / denom)).astype(BF16)
                pv = jnp.dot(p, vv, preferred_element_type=F32)
                o = pv if o is None else o + pv
            outs.append(o.astype(BF16))
        attn = jnp.concatenate(outs, axis=1)
        y = jnp.dot(attn, wo_ref[...], preferred_element_type=F32) + bo_ref[...]
        o_ref[0, pl.ds(r0, BLOCK), :] = x_ref[0, pl.ds(r0, BLOCK), :] + g_ref[0] * y
        return carry

    lax.fori_loop(0, ts // BLOCK, block, 0)
    kv_ref[0:WINDOW, :] = kv_ref[ts:ts + WINDOW, :]


def _attn_layer(x, sh, sc, g, norm_w, w_qkv, b_qkv, w_o, b_o, sinks):
    batch, seq, d = x.shape
    ts = ATTN_TOKENS
    qkv_dim = w_qkv.shape[1]
    vec = pl.BlockSpec((1, 1, d), lambda b, t: (b, 0, 0))
    row = pl.BlockSpec((1, d), lambda b, t: (0, 0))
    tile = pl.BlockSpec((1, ts, d), lambda b, t: (b, t, 0))
    return pl.pallas_call(
        functools.partial(_attn_kernel, ts=ts),
        grid=(batch, seq // ts),
        in_specs=[
            pl.BlockSpec(memory_space=pltpu.SMEM),
            tile, vec, vec, vec, row,
            pl.BlockSpec((d, qkv_dim), lambda b, t: (0, 0)),
            pl.BlockSpec((1, qkv_dim), lambda b, t: (0, 0)),
            pl.BlockSpec((Q_DIM, d), lambda b, t: (0, 0)),
            row,
        ],
        out_specs=tile,
        out_shape=jax.ShapeDtypeStruct(x.shape, F32),
        scratch_shapes=[
            pltpu.VMEM((ts, Q_DIM), BF16),
            pltpu.VMEM((WINDOW + ts, 4 * N_KV_HEADS * LANES), BF16),
            pltpu.VMEM((2, N_HEADS, BLOCK, SPAN), F32),
        ],
        compiler_params=pltpu.CompilerParams(
            dimension_semantics=("arbitrary", "arbitrary"),
            vmem_limit_bytes=VMEM_LIMIT_BYTES),
        name="swa_attn",
    )(sinks, x, sh, sc, g, norm_w, w_qkv, b_qkv, w_o, b_o)


def _conv_kernel(x_ref, sh_ref, sc_ref, g_ref, nw_ref, wpw1_ref, bpw1_ref, wdw_ref, bdw_ref,
                 lng_ref, lnb_ref, wpw2_ref, bpw2_ref, o_ref, u_ref, v_ref, *, ts):
    t = pl.program_id(1)

    @pl.when(t == 0)
    def _():
        u_ref[0:CONV_HALO, :] = jnp.zeros((CONV_HALO, D_MODEL), F32)

    x = x_ref[0]
    h = _modnorm(x, nw_ref[...], sc_ref[0], sh_ref[0])
    u = jnp.dot(h, wpw1_ref[...], preferred_element_type=F32) + bpw1_ref[...]
    u_ref[CONV_HALO:CONV_HALO + ts, :] = u[:, :D_MODEL] * jax.nn.sigmoid(u[:, D_MODEL:])

    first = CONV_HALO - (CONV_WIDTH - 1)
    for r0 in range(0, ts, CONV_ROWS):
        for c0 in range(0, D_MODEL, CONV_COLS):
            cols = slice(c0, c0 + CONV_COLS)
            acc = jnp.broadcast_to(bdw_ref[:, cols], (CONV_ROWS, CONV_COLS))
            for kk in range(CONV_WIDTH):
                rows = slice(first + kk + r0, first + kk + r0 + CONV_ROWS)
                acc = acc + wdw_ref[kk:kk + 1, cols] * u_ref[rows, cols]
            v_ref[r0:r0 + CONV_ROWS, cols] = acc
    u_ref[0:CONV_HALO, :] = u_ref[ts:ts + CONV_HALO, :]

    z = v_ref[...]
    mu = jnp.mean(z, axis=-1, keepdims=True)
    zc = z - mu
    var = jnp.mean(zc * zc, axis=-1, keepdims=True)
    z = zc * lax.rsqrt(var + EPS) * lng_ref[...] + lnb_ref[...]
    z = (z * jax.nn.sigmoid(z)).astype(BF16)
    y = jnp.dot(z, wpw2_ref[...], preferred_element_type=F32) + bpw2_ref[...]
    o_ref[0] = x + g_ref[0] * y


def _conv_layer(x, sh, sc, g, norm_w, w_pw1, b_pw1, w_dw, b_dw, ln_g, ln_b, w_pw2, b_pw2):
    batch, seq, d = x.shape
    ts = CONV_TOKENS
    vec = pl.BlockSpec((1, 1, d), lambda b, t: (b, 0, 0))
    row = pl.BlockSpec((1, d), lambda b, t: (0, 0))
    tile = pl.BlockSpec((1, ts, d), lambda b, t: (b, t, 0))
    return pl.pallas_call(
        functools.partial(_conv_kernel, ts=ts),
        grid=(batch, seq // ts),
        in_specs=[
            tile, vec, vec, vec, row,
            pl.BlockSpec((d, 2 * d), lambda b, t: (0, 0)),
            pl.BlockSpec((1, 2 * d), lambda b, t: (0, 0)),
            pl.BlockSpec((CONV_WIDTH, d), lambda b, t: (0, 0)),
            row, row, row,
            pl.BlockSpec((d, d), lambda b, t: (0, 0)),
            row,
        ],
        out_specs=tile,
        out_shape=jax.ShapeDtypeStruct(x.shape, F32),
        scratch_shapes=[
            pltpu.VMEM((CONV_HALO + ts, d), F32),
            pltpu.VMEM((ts, d), F32),
        ],
        compiler_params=pltpu.CompilerParams(
            dimension_semantics=("arbitrary", "arbitrary"),
            vmem_limit_bytes=VMEM_LIMIT_BYTES),
        name="conformer_conv",
    )(x, sh, sc, g, norm_w, w_pw1, b_pw1, w_dw, b_dw, ln_g, ln_b, w_pw2, b_pw2)


def kernel(x, c, w_mod, b_mod, norm_mix, norm_mlp, w_qkv, b_qkv, w_o, b_o, sinks, w_pw1, b_pw1,
           w_dw, b_dw, conv_ln_g, conv_ln_b, w_pw2, b_pw2, w_up, w_down, final_norm):
    batch, _, d = x.shape
    mod = _modulation(c, w_mod, b_mod)
    mod = mod.reshape(DEPTH, batch, N_MOD, 1, d)
    row = lambda a: a.reshape(1, -1)
    final_w = row(final_norm)
    for i in range(DEPTH):
        sh1, sc1, g1, sh2, sc2, g2 = (mod[i, :, m] for m in range(N_MOD))
        j = i // 2
        if i % 2 == 0:
            x = _attn_layer(x, sh1, sc1, g1, row(norm_mix[i]), w_qkv[j].astype(BF16), row(b_qkv[j]),
                            w_o[j].astype(BF16), row(b_o[j]), sinks[j])
        else:
            x = _conv_layer(x, sh1, sc1, g1, row(norm_mix[i]), w_pw1[j].astype(BF16), row(b_pw1[j]),
                            w_dw[j], row(b_dw[j]), row(conv_ln_g[j]), row(conv_ln_b[j]),
                            w_pw2[j].astype(BF16), row(b_pw2[j]))
        x = _mlp_layer(x, sh2, sc2, g2, row(norm_mlp[i]), w_up[i].astype(BF16),
                       w_down[i].astype(BF16), final_w, final=(i == DEPTH - 1))
    return x
```

```python
import functools

import jax
import jax.numpy as jnp
from jax import lax
from jax.experimental import pallas as pl
from jax.experimental.pallas import tpu as pltpu

D_MODEL = 1024
DEPTH = 4
N_HEADS = 16
N_KV_HEADS = 2
HEAD_DIM = 64
WINDOW = 128
BLOCK = 128
SPAN = BLOCK + WINDOW
Q_DIM = N_HEADS * HEAD_DIM
KV_DIM = N_KV_HEADS * HEAD_DIM
CONV_WIDTH = 31
D_FF = 4 * D_MODEL
N_MOD = 6
EPS = 1e-6

LANES = 128
SUBLANES = 8
CONV_PITCH_PAD = 4
VMEM_LIMIT_BYTES = 56 * 1024 * 1024

MLP_TOKENS = 512
MLP_FF_CHUNK = 1024
ATTN_TOKENS = 512
ATTN_SUB = 256
CONV_TOKENS = 256

F32 = jnp.float32
BF16 = jnp.bfloat16


def _modnorm(x, norm_w, scale, shift):
    r = lax.rsqrt(jnp.mean(x * x, axis=-1, keepdims=True) + EPS)
    return ((x * r) * (norm_w * (1.0 + scale)) + shift).astype(BF16)


def _mod_kernel(c_ref, w_ref, b_ref, o_ref):
    c = c_ref[...]
    cs = c * jax.nn.sigmoid(c)
    o_ref[0] = jnp.dot(cs, w_ref[0], preferred_element_type=F32) + b_ref[0]


def _modulation(c, w_mod, b_mod):
    depth, d, n = w_mod.shape
    batch = c.shape[0]
    nb = D_MODEL
    return pl.pallas_call(
        _mod_kernel,
        grid=(depth, n // nb),
        in_specs=[
            pl.BlockSpec((batch, d), lambda i, j: (0, 0)),
            pl.BlockSpec((1, d, nb), lambda i, j: (i, 0, j)),
            pl.BlockSpec((1, 1, nb), lambda i, j: (i, 0, j)),
        ],
        out_specs=pl.BlockSpec((1, batch, nb), lambda i, j: (i, 0, j)),
        out_shape=jax.ShapeDtypeStruct((depth, batch, n), F32),
        compiler_params=pltpu.CompilerParams(
            dimension_semantics=("arbitrary", "arbitrary"),
            vmem_limit_bytes=VMEM_LIMIT_BYTES),
        name="adaln_mod",
    )(c, w_mod, b_mod.reshape(depth, 1, n))


def _mlp_kernel(x_ref, sh_ref, sc_ref, g_ref, nw_ref, wup_ref, wdn_ref, fn_ref, o_ref, *, final):
    x = x_ref[0]
    h = _modnorm(x, nw_ref[...], sc_ref[0], sh_ref[0])
    acc = jnp.zeros(x.shape, F32)
    for j in range(D_FF // MLP_FF_CHUNK):
        cols = slice(j * MLP_FF_CHUNK, (j + 1) * MLP_FF_CHUNK)
        u = jnp.dot(h, wup_ref[:, cols], preferred_element_type=F32)
        u = jnp.square(jnp.maximum(u, 0.0)).astype(BF16)
        acc = acc + jnp.dot(u, wdn_ref[cols, :], preferred_element_type=F32)
    y = x + g_ref[0] * acc
    if final:
        r = lax.rsqrt(jnp.mean(y * y, axis=-1, keepdims=True) + EPS)
        y = y * r * fn_ref[...]
    o_ref[0] = y


def _mlp_layer(x, sh, sc, g, norm_w, w_up, w_dn, final_w, *, final):
    batch, seq, d = x.shape
    tm = MLP_TOKENS
    vec = pl.BlockSpec((1, 1, d), lambda b, t: (b, 0, 0))
    row = pl.BlockSpec((1, d), lambda b, t: (0, 0))
    tile = pl.BlockSpec((1, tm, d), lambda b, t: (b, t, 0))
    return pl.pallas_call(
        functools.partial(_mlp_kernel, final=final),
        grid=(batch, seq // tm),
        in_specs=[
            tile, vec, vec, vec, row,
            pl.BlockSpec((d, D_FF), lambda b, t: (0, 0), pipeline_mode=pl.Buffered(1)),
            pl.BlockSpec((D_FF, d), lambda b, t: (0, 0), pipeline_mode=pl.Buffered(1)),
            row,
        ],
        out_specs=tile,
        out_shape=jax.ShapeDtypeStruct(x.shape, F32),
        compiler_params=pltpu.CompilerParams(
            dimension_semantics=("arbitrary", "arbitrary"),
            vmem_limit_bytes=VMEM_LIMIT_BYTES),
        name="mlp_final" if final else "mlp",
    )(x, sh, sc, g, norm_w, w_up, w_dn, final_w)


def _alibi_slope(head):
    return 2.0 ** (-8.0 * (head + 1) / N_HEADS)


def _attn_kernel(sinks_ref, x_ref, sh_ref, sc_ref, g_ref, nw_ref, wqkv_ref, bqkv_ref,
                 wo_ref, bo_ref, o_ref, kv_ref, bias_ref, mask_ref, *, ts):
    b = pl.program_id(0)
    t = pl.program_id(1)
    qi = lax.broadcasted_iota(jnp.int32, (BLOCK, BLOCK), 0)
    ki = lax.broadcasted_iota(jnp.int32, (BLOCK, BLOCK), 1)
    cur = ki <= qi

    @pl.when((b == 0) & (t == 0))
    def _():
        dist = jnp.where(cur, qi - ki, qi - ki + WINDOW).astype(F32)
        for hd in range(N_HEADS):
            bias = -_alibi_slope(hd) * dist
            bias_ref[0, hd] = bias
            bias_ref[1, hd] = jnp.where(cur, bias, -jnp.inf)
        mask_ref[0] = jnp.where(cur, 0.0, 1.0).astype(BF16)
        mask_ref[1] = jnp.where(cur, 1.0, 0.0).astype(BF16)

    @pl.when(t == 0)
    def _():
        kv_ref[0:WINDOW, :] = jnp.zeros((WINDOW, kv_ref.shape[1]), BF16)

    low = lax.broadcasted_iota(jnp.int32, (ATTN_SUB, LANES), 1) < HEAD_DIM
    pairs_per_kv = N_HEADS // 2 // N_KV_HEADS
    for sub in range(ts // ATTN_SUB):
        s0 = sub * ATTN_SUB
        x = x_ref[0, s0:s0 + ATTN_SUB, :]
        h = _modnorm(x, nw_ref[...], sc_ref[0], sh_ref[0])
        qkv = jnp.dot(h, wqkv_ref[...], preferred_element_type=F32) + bqkv_ref[...]
        q = (qkv[:, :Q_DIM] * (HEAD_DIM ** -0.5)).astype(BF16)

        copies = []
        for a in (qkv[:, Q_DIM:Q_DIM + KV_DIM], qkv[:, Q_DIM + KV_DIM:]):
            swapped = pltpu.roll(a, HEAD_DIM, 1)
            copies += [jnp.where(low, a, 0.0), jnp.where(low, 0.0, swapped),
                       jnp.where(low, swapped, 0.0), jnp.where(low, 0.0, a)]
        for j, a in enumerate(copies):
            kv_ref[WINDOW + s0:WINDOW + s0 + ATTN_SUB, j * LANES:(j + 1) * LANES] = a.astype(BF16)

        attn_rows = []
        for qb in range(ATTN_SUB // BLOCK):
            r0 = s0 + qb * BLOCK
            table = jnp.where(t == 0, 1, 0) if r0 == 0 else 0
            outs = [None] * (N_HEADS // 2)
            for kvh in range(N_KV_HEADS):
                q4 = jnp.concatenate(
                    [q[qb * BLOCK:(qb + 1) * BLOCK, (kvh * pairs_per_kv + i) * LANES:
                       (kvh * pairs_per_kv + i + 1) * LANES] for i in range(pairs_per_kv)], axis=0)
                o4 = None
                for half in range(2):
                    kcol = (2 * kvh + half) * LANES
                    vcol = (2 * N_KV_HEADS + 2 * kvh + half) * LANES
                    kk = kv_ref[r0:r0 + SPAN, kcol:kcol + LANES]
                    vv = kv_ref[r0:r0 + SPAN, vcol:vcol + LANES]
                    s4 = lax.dot_general(q4, kk, (((1,), (1,)), ((), ())),
                                         preferred_element_type=F32)
                    probs = []
                    for i in range(pairs_per_kv):
                        hd = 2 * (kvh * pairs_per_kv + i) + half
                        sp = s4[i * BLOCK:(i + 1) * BLOCK]
                        s = jnp.where(cur, sp[:, WINDOW:], sp[:, :WINDOW]) + bias_ref[table, hd]
                        sink = sinks_ref[hd]
                        m = jnp.maximum(jnp.max(s, axis=-1, keepdims=True), sink)
                        p = jnp.exp(s - m)
                        denom = jnp.sum(p, axis=-1, keepdims=True) + jnp.exp(sink - m)
                        p = (p * (1.0 / denom)).astype(BF16)
                        probs.append(jnp.concatenate([p * mask_ref[0], p * mask_ref[1]], axis=1))
                    pv = jnp.dot(jnp.concatenate(probs, axis=0), vv, preferred_element_type=F32)
                    o4 = pv if o4 is None else o4 + pv
                for i in range(pairs_per_kv):
                    outs[kvh * pairs_per_kv + i] = o4[i * BLOCK:(i + 1) * BLOCK].astype(BF16)
            attn_rows.append(jnp.concatenate(outs, axis=1))
        attn = jnp.concatenate(attn_rows, axis=0)
        y = jnp.dot(attn, wo_ref[...], preferred_element_type=F32) + bo_ref[...]
        o_ref[0, s0:s0 + ATTN_SUB, :] = x + g_ref[0] * y

    kv_ref[0:WINDOW, :] = kv_ref[ts:ts + WINDOW, :]


def _attn_layer(x, sh, sc, g, norm_w, w_qkv, b_qkv, w_o, b_o, sinks):
    batch, seq, d = x.shape
    ts = ATTN_TOKENS
    qkv_dim = w_qkv.shape[1]
    vec = pl.BlockSpec((1, 1, d), lambda b, t: (b, 0, 0))
    row = pl.BlockSpec((1, d), lambda b, t: (0, 0))
    tile = pl.BlockSpec((1, ts, d), lambda b, t: (b, t, 0))
    return pl.pallas_call(
        functools.partial(_attn_kernel, ts=ts),
        grid=(batch, seq // ts),
        in_specs=[
            pl.BlockSpec(memory_space=pltpu.SMEM),
            tile, vec, vec, vec, row,
            pl.BlockSpec((d, qkv_dim), lambda b, t: (0, 0)),
            pl.BlockSpec((1, qkv_dim), lambda b, t: (0, 0)),
            pl.BlockSpec((Q_DIM, d), lambda b, t: (0, 0)),
            row,
        ],
        out_specs=tile,
        out_shape=jax.ShapeDtypeStruct(x.shape, F32),
        scratch_shapes=[
            pltpu.VMEM((WINDOW + ts, 4 * N_KV_HEADS * LANES), BF16),
            pltpu.VMEM((2, N_HEADS, BLOCK, BLOCK), F32),
            pltpu.VMEM((2, BLOCK, BLOCK), BF16),
        ],
        compiler_params=pltpu.CompilerParams(
            dimension_semantics=("arbitrary", "arbitrary"),
            vmem_limit_bytes=VMEM_LIMIT_BYTES),
        name="swa_attn",
    )(sinks, x, sh, sc, g, norm_w, w_qkv, b_qkv, w_o, b_o)


def _conv_kernel(x_ref, sh_ref, sc_ref, g_ref, nw_ref, wpw1_ref, bpw1_ref, wdw_ref, bdw_ref,
                 lng_ref, lnb_ref, wpw2_ref, bpw2_ref, o_ref, u_ref, v_ref, *, ts):
    t = pl.program_id(1)
    seg = ts // SUBLANES
    pitch = seg + CONV_PITCH_PAD
    n_slabs = D_MODEL // LANES

    @pl.when(t == 0)
    def _():
        u_ref[:, 0:seg, :] = jnp.zeros((n_slabs, seg, LANES), F32)

    x = x_ref[0]
    h = _modnorm(x, nw_ref[...], sc_ref[0], sh_ref[0])
    u = jnp.dot(h, wpw1_ref[...], preferred_element_type=F32) + bpw1_ref[...]
    glu = u[:, :D_MODEL] * jax.nn.sigmoid(u[:, D_MODEL:])
    for a in range(SUBLANES):
        for s in range(n_slabs):
            u_ref[s, (a + 1) * pitch:(a + 1) * pitch + seg, :] = (
                glu[a * seg:(a + 1) * seg, s * LANES:(s + 1) * LANES])

    for s in range(n_slabs):
        cols = slice(s * LANES, (s + 1) * LANES)
        taps = [jnp.broadcast_to(wdw_ref[k:k + 1, cols], (SUBLANES, LANES)) for k in range(CONV_WIDTH)]
        bias = jnp.broadcast_to(bdw_ref[:, cols], (SUBLANES, LANES))
        for j in range(seg):
            acc = bias
            for k in range(CONV_WIDTH):
                src = j + k - (CONV_WIDTH - 1)
                start = pitch + src if src >= 0 else seg + src
                acc = acc + taps[k] * u_ref.at[s][pl.ds(start, SUBLANES, stride=pitch), :]
            v_ref.at[s][pl.ds(j, SUBLANES, stride=pitch), :] = acc
    u_ref[:, 0:seg, :] = u_ref[:, SUBLANES * pitch:SUBLANES * pitch + seg, :]

    z = jnp.concatenate(
        [jnp.concatenate([v_ref[s, a * pitch:a * pitch + seg, :] for a in range(SUBLANES)], axis=0)
         for s in range(n_slabs)], axis=1)
    mu = jnp.mean(z, axis=-1, keepdims=True)
    zc = z - mu
    var = jnp.mean(zc * zc, axis=-1, keepdims=True)
    z = zc * lax.rsqrt(var + EPS) * lng_ref[...] + lnb_ref[...]
    z = (z * jax.nn.sigmoid(z)).astype(BF16)
    y = jnp.dot(z, wpw2_ref[...], preferred_element_type=F32) + bpw2_ref[...]
    o_ref[0] = x + g_ref[0] * y


def _conv_layer(x, sh, sc, g, norm_w, w_pw1, b_pw1, w_dw, b_dw, ln_g, ln_b, w_pw2, b_pw2):
    batch, seq, d = x.shape
    ts = CONV_TOKENS
    pitch = ts // SUBLANES + CONV_PITCH_PAD
    vec = pl.BlockSpec((1, 1, d), lambda b, t: (b, 0, 0))
    row = pl.BlockSpec((1, d), lambda b, t: (0, 0))
    tile = pl.BlockSpec((1, ts, d), lambda b, t: (b, t, 0))
    return pl.pallas_call(
        functools.partial(_conv_kernel, ts=ts),
        grid=(batch, seq // ts),
        in_specs=[
            tile, vec, vec, vec, row,
            pl.BlockSpec((d, 2 * d), lambda b, t: (0, 0)),
            pl.BlockSpec((1, 2 * d), lambda b, t: (0, 0)),
            pl.BlockSpec((CONV_WIDTH, d), lambda b, t: (0, 0)),
            row, row, row,
            pl.BlockSpec((d, d), lambda b, t: (0, 0)),
            row,
        ],
        out_specs=tile,
        out_shape=jax.ShapeDtypeStruct(x.shape, F32),
        scratch_shapes=[
            pltpu.VMEM((d // LANES, (SUBLANES + 1) * pitch, LANES), F32),
            pltpu.VMEM((d // LANES, SUBLANES * pitch, LANES), F32),
        ],
        compiler_params=pltpu.CompilerParams(
            dimension_semantics=("arbitrary", "arbitrary"),
            vmem_limit_bytes=VMEM_LIMIT_BYTES),
        name="conformer_conv",
    )(x, sh, sc, g, norm_w, w_pw1, b_pw1, w_dw, b_dw, ln_g, ln_b, w_pw2, b_pw2)


def kernel(x, c, w_mod, b_mod, norm_mix, norm_mlp, w_qkv, b_qkv, w_o, b_o, sinks, w_pw1, b_pw1,
           w_dw, b_dw, conv_ln_g, conv_ln_b, w_pw2, b_pw2, w_up, w_down, final_norm):
    batch, _, d = x.shape
    mod = _modulation(c, w_mod, b_mod)
    mod = mod.reshape(DEPTH, batch, N_MOD, 1, d)
    row = lambda a: a.reshape(1, -1)
    final_w = row(final_norm)
    for i in range(DEPTH):
        sh1, sc1, g1, sh2, sc2, g2 = (mod[i, :, m] for m in range(N_MOD))
        j = i // 2
        if i % 2 == 0:
            x = _attn_layer(x, sh1, sc1, g1, row(norm_mix[i]), w_qkv[j].astype(BF16), row(b_qkv[j]),
                            w_o[j].astype(BF16), row(b_o[j]), sinks[j])
        else:
            x = _conv_layer(x, sh1, sc1, g1, row(norm_mix[i]), w_pw1[j].astype(BF16), row(b_pw1[j]),
                            w_dw[j], row(b_dw[j]), row(conv_ln_g[j]), row(conv_ln_b[j]),
                            w_pw2[j].astype(BF16), row(b_pw2[j]))
        x = _mlp_layer(x, sh2, sc2, g2, row(norm_mlp[i]), w_up[i].astype(BF16),
                       w_down[i].astype(BF16), final_w, final=(i == DEPTH - 1))
    return x
```

```python
import functools

import jax
import jax.numpy as jnp
from jax import lax
from jax.experimental import pallas as pl
from jax.experimental.pallas import tpu as pltpu

D_MODEL = 1024
DEPTH = 4
N_HEADS = 16
N_KV_HEADS = 2
HEAD_DIM = 64
GROUP = N_HEADS // N_KV_HEADS
WINDOW = 128
BLOCK = 128
SPAN = BLOCK + WINDOW
Q_DIM = N_HEADS * HEAD_DIM
KV_DIM = N_KV_HEADS * HEAD_DIM
CONV_WIDTH = 31
D_FF = 4 * D_MODEL
N_MOD = 6
EPS = 1e-6

LANES = 128
SUBLANES = 8
CONV_PITCH_PAD = 4
VMEM_LIMIT_BYTES = 56 * 1024 * 1024

MLP_TOKENS = 512
MLP_FF_CHUNK = 1024
ATTN_TOKENS = 512
ATTN_SUB = 256
ATTN_COLS = 256
CONV_TOKENS = 256
CONV_COLS = 256

F32 = jnp.float32
BF16 = jnp.bfloat16


def _modnorm(x, norm_w, scale, shift):
    r = lax.rsqrt(jnp.mean(x * x, axis=-1, keepdims=True) + EPS)
    return ((x * r) * (norm_w * (1.0 + scale)) + shift).astype(BF16)


def _mod_kernel(c_ref, w_ref, b_ref, o_ref):
    c = c_ref[...]
    cs = c * jax.nn.sigmoid(c)
    o_ref[0] = jnp.dot(cs, w_ref[0], preferred_element_type=F32) + b_ref[0]


def _modulation(c, w_mod, b_mod):
    depth, d, n = w_mod.shape
    batch = c.shape[0]
    nb = D_MODEL
    return pl.pallas_call(
        _mod_kernel,
        grid=(depth, n // nb),
        in_specs=[
            pl.BlockSpec((batch, d), lambda i, j: (0, 0)),
            pl.BlockSpec((1, d, nb), lambda i, j: (i, 0, j)),
            pl.BlockSpec((1, 1, nb), lambda i, j: (i, 0, j)),
        ],
        out_specs=pl.BlockSpec((1, batch, nb), lambda i, j: (i, 0, j)),
        out_shape=jax.ShapeDtypeStruct((depth, batch, n), F32),
        compiler_params=pltpu.CompilerParams(
            dimension_semantics=("arbitrary", "arbitrary"),
            vmem_limit_bytes=VMEM_LIMIT_BYTES),
        name="adaln_mod",
    )(c, w_mod, b_mod.reshape(depth, 1, n))


def _mlp_kernel(x_ref, sh_ref, sc_ref, g_ref, nw_ref, wup_ref, wdn_ref, fn_ref, o_ref, *, final):
    x = x_ref[0]
    h = _modnorm(x, nw_ref[...], sc_ref[0], sh_ref[0])
    acc = jnp.zeros(x.shape, F32)
    for j in range(D_FF // MLP_FF_CHUNK):
        cols = slice(j * MLP_FF_CHUNK, (j + 1) * MLP_FF_CHUNK)
        u = jnp.dot(h, wup_ref[0, :, cols], preferred_element_type=F32)
        u = jnp.square(jnp.maximum(u, 0.0)).astype(BF16)
        acc = acc + jnp.dot(u, wdn_ref[0, cols, :], preferred_element_type=F32)
    y = x + g_ref[0] * acc
    if final:
        r = lax.rsqrt(jnp.mean(y * y, axis=-1, keepdims=True) + EPS)
        y = y * r * fn_ref[...]
    o_ref[0] = y


def _mlp_layer(x, sh, sc, g, norm_w, w_up, w_dn, final_w, *, layer, final):
    batch, seq, d = x.shape
    tm = MLP_TOKENS
    vec = pl.BlockSpec((1, 1, d), lambda b, t: (b, 0, 0))
    row = pl.BlockSpec((1, d), lambda b, t: (0, 0))
    tile = pl.BlockSpec((1, tm, d), lambda b, t: (b, t, 0))
    return pl.pallas_call(
        functools.partial(_mlp_kernel, final=final),
        grid=(batch, seq // tm),
        in_specs=[
            tile, vec, vec, vec, row,
            pl.BlockSpec((1, d, D_FF), lambda b, t: (layer, 0, 0), pipeline_mode=pl.Buffered(1)),
            pl.BlockSpec((1, D_FF, d), lambda b, t: (layer, 0, 0), pipeline_mode=pl.Buffered(1)),
            row,
        ],
        out_specs=tile,
        out_shape=jax.ShapeDtypeStruct(x.shape, F32),
        compiler_params=pltpu.CompilerParams(
            dimension_semantics=("arbitrary", "arbitrary"),
            vmem_limit_bytes=VMEM_LIMIT_BYTES),
        name="mlp_final" if final else "mlp",
    )(x, sh, sc, g, norm_w, w_up, w_dn, final_w)


def _alibi_slope(head):
    return 2.0 ** (-8.0 * (head + 1) / N_HEADS)


def _attn_kernel(sinks_ref, x_ref, sh_ref, sc_ref, g_ref, nw_ref, wqk_ref, bqk_ref, wvt_ref, bvt_ref,
                 wo_ref, bo_ref, o_ref, k_ref, vt_ref, bias_ref, mask_ref, *, ts):
    b = pl.program_id(0)
    t = pl.program_id(1)
    si = lax.broadcasted_iota(jnp.int32, (BLOCK, BLOCK), 0)
    qi = lax.broadcasted_iota(jnp.int32, (BLOCK, BLOCK), 1)
    cur = si <= qi

    @pl.when((b == 0) & (t == 0))
    def _():
        dist = jnp.where(cur, qi - si, qi - si + WINDOW).astype(F32)
        for hd in range(N_HEADS):
            bias = -_alibi_slope(hd) * dist
            bias_ref[0, hd] = bias
            bias_ref[1, hd] = jnp.where(cur, bias, -jnp.inf)
        mask_ref[0] = jnp.where(cur, 0.0, 1.0).astype(BF16)
        mask_ref[1] = jnp.where(cur, 1.0, 0.0).astype(BF16)

    @pl.when(t == 0)
    def _():
        k_ref[0:WINDOW, :] = jnp.zeros((WINDOW, k_ref.shape[1]), BF16)
        vt_ref[:, 0:WINDOW] = jnp.zeros((KV_DIM, WINDOW), BF16)

    low = lax.broadcasted_iota(jnp.int32, (ATTN_SUB, LANES), 1) < HEAD_DIM
    nt = (((1,), (1,)), ((), ()))
    n_sub = ts // ATTN_SUB
    n_chunks = Q_DIM // ATTN_COLS
    groups = [(sub, qb, kvh) for sub in range(n_sub) for qb in range(ATTN_SUB // BLOCK)
              for kvh in range(N_KV_HEADS)]
    per_sub = len(groups) // n_sub
    xs, hs, q_parts, scores, head_out, attn = {}, {}, {}, {}, {}, {}

    def norm(sub):
        xs[sub] = x_ref[0, sub * ATTN_SUB:(sub + 1) * ATTN_SUB, :]
        hs[sub] = _modnorm(xs[sub], nw_ref[...], sc_ref[0], sh_ref[0])
        q_parts[sub] = [None] * n_chunks

    def project(sub, c):
        s0 = sub * ATTN_SUB
        last = c == n_chunks - 1
        cols = slice(c * ATTN_COLS, Q_DIM + KV_DIM if last else (c + 1) * ATTN_COLS)
        qk = jnp.dot(hs[sub], wqk_ref[:, cols], preferred_element_type=F32) + bqk_ref[:, cols]
        q_parts[sub][c] = (qk[:, :ATTN_COLS] * (HEAD_DIM ** -0.5)).astype(BF16)
        if last:
            kf = qk[:, ATTN_COLS:]
            swapped = pltpu.roll(kf, HEAD_DIM, 1)
            copies = [jnp.where(low, kf, 0.0), jnp.where(low, 0.0, swapped),
                      jnp.where(low, swapped, 0.0), jnp.where(low, 0.0, kf)]
            for j, a in enumerate(copies):
                k_ref[WINDOW + s0:WINDOW + s0 + ATTN_SUB, j * LANES:(j + 1) * LANES] = a.astype(BF16)
            vt = lax.dot_general(wvt_ref[...], hs[sub], nt, preferred_element_type=F32) + bvt_ref[...]
            vt_ref[:, WINDOW + s0:WINDOW + s0 + ATTN_SUB] = vt.astype(BF16)

    def score(g):
        sub, qb, kvh = groups[g]
        r0 = sub * ATTN_SUB + qb * BLOCK
        rows = slice(qb * BLOCK, (qb + 1) * BLOCK)
        pairs = [q_parts[sub][(kvh * GROUP // 2 + i) * LANES // ATTN_COLS]
                 [rows, ((kvh * GROUP // 2 + i) * LANES) % ATTN_COLS:
                        ((kvh * GROUP // 2 + i) * LANES) % ATTN_COLS + LANES] for i in range(GROUP // 2)]
        q4 = jnp.concatenate(pairs, axis=0)
        scores[g] = [lax.dot_general(k_ref[r0:r0 + SPAN, (2 * kvh + half) * LANES:(2 * kvh + half + 1) * LANES],
                                     q4, nt, preferred_element_type=F32) for half in range(2)]

    def attend(g):
        sub, qb, kvh = groups[g]
        r0 = sub * ATTN_SUB + qb * BLOCK
        table = jnp.where(t == 0, 1, 0) if r0 == 0 else 0
        probs = [None] * GROUP
        for half in range(2):
            st = scores[g][half]
            for i in range(GROUP // 2):
                hd = kvh * GROUP + 2 * i + half
                sp = st[:, i * BLOCK:(i + 1) * BLOCK]
                s = jnp.where(cur, sp[WINDOW:], sp[:WINDOW]) + bias_ref[table, hd]
                sink = sinks_ref[hd]
                m = jnp.maximum(jnp.max(s, axis=0, keepdims=True), sink)
                p = jnp.exp(s - m)
                denom = jnp.sum(p, axis=0, keepdims=True) + jnp.exp(sink - m)
                p = (p * (1.0 / denom)).astype(BF16)
                probs[2 * i + half] = jnp.concatenate([p * mask_ref[0], p * mask_ref[1]], axis=0)
        del scores[g]
        vspan = vt_ref[kvh * HEAD_DIM:(kvh + 1) * HEAD_DIM, r0:r0 + SPAN]
        ot = jnp.dot(vspan, jnp.concatenate(probs, axis=1), preferred_element_type=F32)
        for j in range(GROUP):
            head_out[(sub, qb, kvh * GROUP + j)] = ot[:, j * BLOCK:(j + 1) * BLOCK].astype(BF16)

    def gather(sub):
        attn_t = jnp.concatenate(
            [jnp.concatenate([head_out.pop((sub, qb, hd)) for hd in range(N_HEADS)], axis=0)
             for qb in range(ATTN_SUB // BLOCK)], axis=1)
        attn[sub] = attn_t.T

    def output(sub, c):
        cols = slice(c * ATTN_COLS, (c + 1) * ATTN_COLS)
        y = jnp.dot(attn[sub], wo_ref[:, cols], preferred_element_type=F32) + bo_ref[:, cols]
        o_ref[0, sub * ATTN_SUB:(sub + 1) * ATTN_SUB, cols] = xs[sub][:, cols] + g_ref[0][:, cols] * y

    norm(0)
    for c in range(n_chunks):
        project(0, c)
    score(0)
    for g in range(len(groups)):
        sub, j = divmod(g, per_sub)
        if sub + 1 < n_sub:
            if j == 0:
                norm(sub + 1)
            project(sub + 1, j)
        if sub >= 1:
            output(sub - 1, j)
        if g + 1 < len(groups):
            score(g + 1)
        attend(g)
        if j == per_sub - 1:
            gather(sub)
    for c in range(n_chunks):
        output(n_sub - 1, c)

    k_ref[0:WINDOW, :] = k_ref[ts:ts + WINDOW, :]
    vt_ref[:, 0:WINDOW] = vt_ref[:, ts:ts + WINDOW]


def _attn_layer(x, sh, sc, g, norm_w, w_qkv, b_qkv, w_o, b_o, sinks):
    batch, seq, d = x.shape
    ts = ATTN_TOKENS
    qk_dim = Q_DIM + KV_DIM
    w_qk = w_qkv[:, :qk_dim].astype(BF16)
    w_vt = w_qkv[:, qk_dim:].T.astype(BF16)
    b_qk = b_qkv[:qk_dim].reshape(1, qk_dim)
    b_vt = b_qkv[qk_dim:].reshape(KV_DIM, 1)
    vec = pl.BlockSpec((1, 1, d), lambda b, t: (b, 0, 0))
    row = pl.BlockSpec((1, d), lambda b, t: (0, 0))
    tile = pl.BlockSpec((1, ts, d), lambda b, t: (b, t, 0))
    return pl.pallas_call(
        functools.partial(_attn_kernel, ts=ts),
        grid=(batch, seq // ts),
        in_specs=[
            pl.BlockSpec(memory_space=pltpu.SMEM),
            tile, vec, vec, vec, row,
            pl.BlockSpec((d, qk_dim), lambda b, t: (0, 0)),
            pl.BlockSpec((1, qk_dim), lambda b, t: (0, 0)),
            pl.BlockSpec((KV_DIM, d), lambda b, t: (0, 0)),
            pl.BlockSpec((KV_DIM, 1), lambda b, t: (0, 0)),
            pl.BlockSpec((Q_DIM, d), lambda b, t: (0, 0)),
            row,
        ],
        out_specs=tile,
        out_shape=jax.ShapeDtypeStruct(x.shape, F32),
        scratch_shapes=[
            pltpu.VMEM((WINDOW + ts, 2 * N_KV_HEADS * LANES), BF16),
            pltpu.VMEM((KV_DIM, WINDOW + ts), BF16),
            pltpu.VMEM((2, N_HEADS, BLOCK, BLOCK), F32),
            pltpu.VMEM((2, BLOCK, BLOCK), BF16),
        ],
        compiler_params=pltpu.CompilerParams(
            dimension_semantics=("arbitrary", "arbitrary"),
            vmem_limit_bytes=VMEM_LIMIT_BYTES),
        name="swa_attn",
    )(sinks, x, sh, sc, g, norm_w, w_qk, b_qk, w_vt, b_vt, w_o.astype(BF16), b_o.reshape(1, d))


def _conv_kernel(x_ref, sh_ref, sc_ref, g_ref, nw_ref, wpw1_ref, bpw1_ref, wdw_ref, bdw_ref,
                 lng_ref, lnb_ref, wpw2_ref, bpw2_ref, o_ref, u_ref, v_ref, *, ts):
    t = pl.program_id(1)
    seg = ts // SUBLANES
    pitch = seg + CONV_PITCH_PAD
    n_slabs = D_MODEL // LANES

    @pl.when(t == 0)
    def _():
        u_ref[:, 0:seg, :] = jnp.zeros((n_slabs, seg, LANES), F32)

    x = x_ref[0]
    h = _modnorm(x, nw_ref[...], sc_ref[0], sh_ref[0])
    slabs_per_chunk = CONV_COLS // LANES

    def gate(c):
        cols = slice(c * CONV_COLS, (c + 1) * CONV_COLS)
        gcols = slice(D_MODEL + c * CONV_COLS, D_MODEL + (c + 1) * CONV_COLS)
        ua = jnp.dot(h, wpw1_ref[:, cols], preferred_element_type=F32) + bpw1_ref[:, cols]
        ug = jnp.dot(h, wpw1_ref[:, gcols], preferred_element_type=F32) + bpw1_ref[:, gcols]
        glu = ua * jax.nn.sigmoid(ug)
        for a in range(SUBLANES):
            for i in range(slabs_per_chunk):
                u_ref[c * slabs_per_chunk + i, (a + 1) * pitch:(a + 1) * pitch + seg, :] = (
                    glu[a * seg:(a + 1) * seg, i * LANES:(i + 1) * LANES])

    def conv(s):
        cols = slice(s * LANES, (s + 1) * LANES)
        taps = [jnp.broadcast_to(wdw_ref[k:k + 1, cols], (SUBLANES, LANES)) for k in range(CONV_WIDTH)]
        bias = jnp.broadcast_to(bdw_ref[:, cols], (SUBLANES, LANES))
        for j in range(seg):
            acc = bias
            for k in range(CONV_WIDTH):
                src = j + k - (CONV_WIDTH - 1)
                start = pitch + src if src >= 0 else seg + src
                acc = acc + taps[k] * u_ref.at[s][pl.ds(start, SUBLANES, stride=pitch), :]
            v_ref.at[s][pl.ds(j, SUBLANES, stride=pitch), :] = acc
        u_ref[s, 0:seg, :] = u_ref[s, SUBLANES * pitch:SUBLANES * pitch + seg, :]

    n_chunks = D_MODEL // CONV_COLS
    gate(0)
    for c in range(n_chunks):
        if c + 1 < n_chunks:
            gate(c + 1)
        for i in range(slabs_per_chunk):
            conv(c * slabs_per_chunk + i)

    z = jnp.concatenate(
        [jnp.concatenate([v_ref[s, a * pitch:a * pitch + seg, :] for a in range(SUBLANES)], axis=0)
         for s in range(n_slabs)], axis=1)
    mu = jnp.mean(z, axis=-1, keepdims=True)
    zc = z - mu
    var = jnp.mean(zc * zc, axis=-1, keepdims=True)
    z = zc * lax.rsqrt(var + EPS) * lng_ref[...] + lnb_ref[...]
    z = (z * jax.nn.sigmoid(z)).astype(BF16)
    y = jnp.dot(z, wpw2_ref[...], preferred_element_type=F32) + bpw2_ref[...]
    o_ref[0] = x + g_ref[0] * y


def _conv_layer(x, sh, sc, g, norm_w, w_pw1, b_pw1, w_dw, b_dw, ln_g, ln_b, w_pw2, b_pw2):
    batch, seq, d = x.shape
    ts = CONV_TOKENS
    pitch = ts // SUBLANES + CONV_PITCH_PAD
    vec = pl.BlockSpec((1, 1, d), lambda b, t: (b, 0, 0))
    row = pl.BlockSpec((1, d), lambda b, t: (0, 0))
    tile = pl.BlockSpec((1, ts, d), lambda b, t: (b, t, 0))
    return pl.pallas_call(
        functools.partial(_conv_kernel, ts=ts),
        grid=(batch, seq // ts),
        in_specs=[
            tile, vec, vec, vec, row,
            pl.BlockSpec((d, 2 * d), lambda b, t: (0, 0)),
            pl.BlockSpec((1, 2 * d), lambda b, t: (0, 0)),
            pl.BlockSpec((CONV_WIDTH, d), lambda b, t: (0, 0)),
            row, row, row,
            pl.BlockSpec((d, d), lambda b, t: (0, 0)),
            row,
        ],
        out_specs=tile,
        out_shape=jax.ShapeDtypeStruct(x.shape, F32),
        scratch_shapes=[
            pltpu.VMEM((d // LANES, (SUBLANES + 1) * pitch, LANES), F32),
            pltpu.VMEM((d // LANES, SUBLANES * pitch, LANES), F32),
        ],
        compiler_params=pltpu.CompilerParams(
            dimension_semantics=("arbitrary", "arbitrary"),
            vmem_limit_bytes=VMEM_LIMIT_BYTES),
        name="conformer_conv",
    )(x, sh, sc, g, norm_w, w_pw1, b_pw1, w_dw, b_dw, ln_g, ln_b, w_pw2, b_pw2)


def kernel(x, c, w_mod, b_mod, norm_mix, norm_mlp, w_qkv, b_qkv, w_o, b_o, sinks, w_pw1, b_pw1,
           w_dw, b_dw, conv_ln_g, conv_ln_b, w_pw2, b_pw2, w_up, w_down, final_norm):
    batch, _, d = x.shape
    mod = _modulation(c, w_mod, b_mod)
    mod = mod.reshape(DEPTH, batch, N_MOD, 1, d)
    row = lambda a: a.reshape(1, -1)
    final_w = row(final_norm)
    w_up = w_up.astype(BF16)
    w_down = w_down.astype(BF16)
    for i in range(DEPTH):
        sh1, sc1, g1, sh2, sc2, g2 = (mod[i, :, m] for m in range(N_MOD))
        j = i // 2
        if i % 2 == 0:
            x = _attn_layer(x, sh1, sc1, g1, row(norm_mix[i]), w_qkv[j], b_qkv[j], w_o[j], b_o[j],
                            sinks[j])
        else:
            x = _conv_layer(x, sh1, sc1, g1, row(norm_mix[i]), w_pw1[j].astype(BF16), row(b_pw1[j]),
                            w_dw[j], row(b_dw[j]), row(conv_ln_g[j]), row(conv_ln_b[j]),
                            w_pw2[j].astype(BF16), row(b_pw2[j]))
        x = _mlp_layer(x, sh2, sc2, g2, row(norm_mlp[i]), w_up, w_down, final_w,
                       layer=i, final=(i == DEPTH - 1))
    return x
```

```python
import functools

import jax
import jax.numpy as jnp
from jax import lax
from jax.experimental import pallas as pl
from jax.experimental.pallas import tpu as pltpu

D_MODEL = 1024
DEPTH = 4
N_HEADS = 16
N_KV_HEADS = 2
HEAD_DIM = 64
GROUP = N_HEADS // N_KV_HEADS
WINDOW = 128
BLOCK = 128
SPAN = BLOCK + WINDOW
Q_DIM = N_HEADS * HEAD_DIM
KV_DIM = N_KV_HEADS * HEAD_DIM
CONV_WIDTH = 31
D_FF = 4 * D_MODEL
N_MOD = 6
EPS = 1e-6

LANES = 128
SUBLANES = 8
CONV_PITCH_PAD = 4
VMEM_LIMIT_BYTES = 56 * 1024 * 1024

MLP_TOKENS = 1024
MLP_SUB = 512
MLP_FF_CHUNK = 1024
ATTN_TOKENS = 1024
ATTN_SUB = 256
ATTN_COLS = 256
CONV_TOKENS = 256
CONV_COLS = 256

F32 = jnp.float32
BF16 = jnp.bfloat16


def _modnorm(x, norm_w, scale, shift):
    r = lax.rsqrt(jnp.mean(x * x, axis=-1, keepdims=True) + EPS)
    return ((x * r) * (norm_w * (1.0 + scale)) + shift).astype(BF16)


def _mod_kernel(c_ref, w_ref, b_ref, o_ref):
    c = c_ref[...]
    cs = c * jax.nn.sigmoid(c)
    o_ref[0] = jnp.dot(cs, w_ref[0], preferred_element_type=F32) + b_ref[0]


def _modulation(c, w_mod, b_mod):
    depth, d, n = w_mod.shape
    batch = c.shape[0]
    nb = D_MODEL
    return pl.pallas_call(
        _mod_kernel,
        grid=(depth, n // nb),
        in_specs=[
            pl.BlockSpec((batch, d), lambda i, j: (0, 0)),
            pl.BlockSpec((1, d, nb), lambda i, j: (i, 0, j)),
            pl.BlockSpec((1, 1, nb), lambda i, j: (i, 0, j)),
        ],
        out_specs=pl.BlockSpec((1, batch, nb), lambda i, j: (i, 0, j)),
        out_shape=jax.ShapeDtypeStruct((depth, batch, n), F32),
        compiler_params=pltpu.CompilerParams(
            dimension_semantics=("arbitrary", "arbitrary"),
            vmem_limit_bytes=VMEM_LIMIT_BYTES),
        name="adaln_mod",
    )(c, w_mod, b_mod.reshape(depth, 1, n))


def _mlp_kernel(x_ref, sh_ref, sc_ref, g_ref, nw_ref, wup_ref, wdn_ref, fn_ref, o_ref, *, tm, final):
    n_sub = tm // MLP_SUB
    n_chunks = D_FF // MLP_FF_CHUNK
    xs, hs = {}, {}

    def norm(sub):
        xs[sub] = x_ref[0, sub * MLP_SUB:(sub + 1) * MLP_SUB, :]
        hs[sub] = _modnorm(xs[sub], nw_ref[...], sc_ref[0], sh_ref[0])

    def up(sub, j):
        cols = slice(j * MLP_FF_CHUNK, (j + 1) * MLP_FF_CHUNK)
        return jnp.dot(hs[sub], wup_ref[0, :, cols], preferred_element_type=F32)

    def ffn(sub):
        x = xs.pop(sub)
        acc = jnp.zeros(x.shape, F32)
        u_next = up(sub, 0)
        for j in range(n_chunks):
            u = u_next
            if j + 1 < n_chunks:
                u_next = up(sub, j + 1)
            rows = slice(j * MLP_FF_CHUNK, (j + 1) * MLP_FF_CHUNK)
            act = jnp.square(jnp.maximum(u, 0.0)).astype(BF16)
            acc = acc + jnp.dot(act, wdn_ref[0, rows, :], preferred_element_type=F32)
        y = x + g_ref[0] * acc
        if final:
            r = lax.rsqrt(jnp.mean(y * y, axis=-1, keepdims=True) + EPS)
            y = y * r * fn_ref[...]
        o_ref[0, sub * MLP_SUB:(sub + 1) * MLP_SUB, :] = y

    norm(0)
    for sub in range(n_sub):
        if sub + 1 < n_sub:
            norm(sub + 1)
        ffn(sub)


def _mlp_layer(x, sh, sc, g, norm_w, w_up, w_dn, final_w, *, layer, final):
    batch, seq, d = x.shape
    tm = MLP_TOKENS
    vec = pl.BlockSpec((1, 1, d), lambda b, t: (b, 0, 0))
    row = pl.BlockSpec((1, d), lambda b, t: (0, 0))
    tile = pl.BlockSpec((1, tm, d), lambda b, t: (b, t, 0))
    return pl.pallas_call(
        functools.partial(_mlp_kernel, tm=tm, final=final),
        grid=(batch, seq // tm),
        in_specs=[
            tile, vec, vec, vec, row,
            pl.BlockSpec((1, d, D_FF), lambda b, t: (layer, 0, 0), pipeline_mode=pl.Buffered(1)),
            pl.BlockSpec((1, D_FF, d), lambda b, t: (layer, 0, 0), pipeline_mode=pl.Buffered(1)),
            row,
        ],
        out_specs=tile,
        out_shape=jax.ShapeDtypeStruct(x.shape, F32),
        compiler_params=pltpu.CompilerParams(
            dimension_semantics=("arbitrary", "arbitrary"),
            vmem_limit_bytes=VMEM_LIMIT_BYTES),
        name="mlp_final" if final else "mlp",
    )(x, sh, sc, g, norm_w, w_up, w_dn, final_w)


def _alibi_slope(head):
    return 2.0 ** (-8.0 * (head + 1) / N_HEADS)


def _attn_kernel(sinks_ref, x_ref, sh_ref, sc_ref, g_ref, nw_ref, wqk_ref, bqk_ref, wvt_ref, bvt_ref,
                 wo_ref, bo_ref, o_ref, k_ref, vt_ref, bias_ref, mask_ref, *, ts):
    b = pl.program_id(0)
    t = pl.program_id(1)
    si = lax.broadcasted_iota(jnp.int32, (BLOCK, BLOCK), 0)
    qi = lax.broadcasted_iota(jnp.int32, (BLOCK, BLOCK), 1)
    cur = si <= qi

    @pl.when((b == 0) & (t == 0))
    def _():
        dist = jnp.where(cur, qi - si, qi - si + WINDOW).astype(F32)
        for hd in range(N_HEADS):
            bias = -_alibi_slope(hd) * dist
            bias_ref[0, hd] = bias
            bias_ref[1, hd] = jnp.where(cur, bias, -jnp.inf)
        mask_ref[0] = jnp.where(cur, 0.0, 1.0).astype(BF16)
        mask_ref[1] = jnp.where(cur, 1.0, 0.0).astype(BF16)

    @pl.when(t == 0)
    def _():
        k_ref[0:WINDOW, :] = jnp.zeros((WINDOW, k_ref.shape[1]), BF16)
        vt_ref[:, 0:WINDOW] = jnp.zeros((KV_DIM, WINDOW), BF16)

    low = lax.broadcasted_iota(jnp.int32, (ATTN_SUB, LANES), 1) < HEAD_DIM
    nt = (((1,), (1,)), ((), ()))
    n_sub = ts // ATTN_SUB
    n_chunks = Q_DIM // ATTN_COLS
    groups = [(sub, qb, kvh) for sub in range(n_sub) for qb in range(ATTN_SUB // BLOCK)
              for kvh in range(N_KV_HEADS)]
    per_sub = len(groups) // n_sub
    xs, hs, q_parts, scores, head_out, attn = {}, {}, {}, {}, {}, {}

    def norm(sub):
        xs[sub] = x_ref[0, sub * ATTN_SUB:(sub + 1) * ATTN_SUB, :]
        hs[sub] = _modnorm(xs[sub], nw_ref[...], sc_ref[0], sh_ref[0])
        q_parts[sub] = [None] * n_chunks

    def project(sub, c):
        s0 = sub * ATTN_SUB
        last = c == n_chunks - 1
        cols = slice(c * ATTN_COLS, Q_DIM + KV_DIM if last else (c + 1) * ATTN_COLS)
        qk = jnp.dot(hs[sub], wqk_ref[:, cols], preferred_element_type=F32) + bqk_ref[:, cols]
        q_parts[sub][c] = (qk[:, :ATTN_COLS] * (HEAD_DIM ** -0.5)).astype(BF16)
        if last:
            kf = qk[:, ATTN_COLS:]
            swapped = pltpu.roll(kf, HEAD_DIM, 1)
            copies = [jnp.where(low, kf, 0.0), jnp.where(low, 0.0, swapped),
                      jnp.where(low, swapped, 0.0), jnp.where(low, 0.0, kf)]
            for j, a in enumerate(copies):
                k_ref[WINDOW + s0:WINDOW + s0 + ATTN_SUB, j * LANES:(j + 1) * LANES] = a.astype(BF16)
            vt = lax.dot_general(wvt_ref[...], hs[sub], nt, preferred_element_type=F32) + bvt_ref[...]
            vt_ref[:, WINDOW + s0:WINDOW + s0 + ATTN_SUB] = vt.astype(BF16)

    def score(g):
        sub, qb, kvh = groups[g]
        r0 = sub * ATTN_SUB + qb * BLOCK
        rows = slice(qb * BLOCK, (qb + 1) * BLOCK)
        pairs = [q_parts[sub][(kvh * GROUP // 2 + i) * LANES // ATTN_COLS]
                 [rows, ((kvh * GROUP // 2 + i) * LANES) % ATTN_COLS:
                        ((kvh * GROUP // 2 + i) * LANES) % ATTN_COLS + LANES] for i in range(GROUP // 2)]
        q4 = jnp.concatenate(pairs, axis=0)
        scores[g] = [lax.dot_general(k_ref[r0:r0 + SPAN, (2 * kvh + half) * LANES:(2 * kvh + half + 1) * LANES],
                                     q4, nt, preferred_element_type=F32) for half in range(2)]

    def attend(g):
        sub, qb, kvh = groups[g]
        r0 = sub * ATTN_SUB + qb * BLOCK
        table = jnp.where(t == 0, 1, 0) if r0 == 0 else 0
        probs = [None] * GROUP
        for half in range(2):
            st = scores[g][half]
            for i in range(GROUP // 2):
                hd = kvh * GROUP + 2 * i + half
                sp = st[:, i * BLOCK:(i + 1) * BLOCK]
                s = jnp.where(cur, sp[WINDOW:], sp[:WINDOW]) + bias_ref[table, hd]
                sink = sinks_ref[hd]
                m = jnp.maximum(jnp.max(s, axis=0, keepdims=True), sink)
                p = jnp.exp(s - m)
                denom = jnp.sum(p, axis=0, keepdims=True) + jnp.exp(sink - m)
                p = (p * (1.0 / denom)).astype(BF16)
                probs[2 * i + half] = jnp.concatenate([p * mask_ref[0], p * mask_ref[1]], axis=0)
        del scores[g]
        vspan = vt_ref[kvh * HEAD_DIM:(kvh + 1) * HEAD_DIM, r0:r0 + SPAN]
        ot = jnp.dot(vspan, jnp.concatenate(probs, axis=1), preferred_element_type=F32)
        for j in range(GROUP):
            head_out[(sub, qb, kvh * GROUP + j)] = ot[:, j * BLOCK:(j + 1) * BLOCK].astype(BF16)

    def gather(sub):
        attn_t = jnp.concatenate(
            [jnp.concatenate([head_out.pop((sub, qb, hd)) for hd in range(N_HEADS)], axis=0)
             for qb in range(ATTN_SUB // BLOCK)], axis=1)
        attn[sub] = attn_t.T

    def output(sub, c):
        cols = slice(c * ATTN_COLS, (c + 1) * ATTN_COLS)
        y = jnp.dot(attn[sub], wo_ref[:, cols], preferred_element_type=F32) + bo_ref[:, cols]
        o_ref[0, sub * ATTN_SUB:(sub + 1) * ATTN_SUB, cols] = xs[sub][:, cols] + g_ref[0][:, cols] * y

    norm(0)
    for c in range(n_chunks):
        project(0, c)
    score(0)
    for g in range(len(groups)):
        sub, j = divmod(g, per_sub)
        if sub + 1 < n_sub:
            if j == 0:
                norm(sub + 1)
            project(sub + 1, j)
        if sub >= 1:
            output(sub - 1, j)
        if g + 1 < len(groups):
            score(g + 1)
        attend(g)
        if j == per_sub - 1:
            gather(sub)
    for c in range(n_chunks):
        output(n_sub - 1, c)

    k_ref[0:WINDOW, :] = k_ref[ts:ts + WINDOW, :]
    vt_ref[:, 0:WINDOW] = vt_ref[:, ts:ts + WINDOW]


def _attn_layer(x, sh, sc, g, norm_w, w_qkv, b_qkv, w_o, b_o, sinks):
    batch, seq, d = x.shape
    ts = ATTN_TOKENS
    qk_dim = Q_DIM + KV_DIM
    w_qk = w_qkv[:, :qk_dim].astype(BF16)
    w_vt = w_qkv[:, qk_dim:].T.astype(BF16)
    b_qk = b_qkv[:qk_dim].reshape(1, qk_dim)
    b_vt = b_qkv[qk_dim:].reshape(KV_DIM, 1)
    vec = pl.BlockSpec((1, 1, d), lambda b, t: (b, 0, 0))
    row = pl.BlockSpec((1, d), lambda b, t: (0, 0))
    tile = pl.BlockSpec((1, ts, d), lambda b, t: (b, t, 0))
    return pl.pallas_call(
        functools.partial(_attn_kernel, ts=ts),
        grid=(batch, seq // ts),
        in_specs=[
            pl.BlockSpec(memory_space=pltpu.SMEM),
            tile, vec, vec, vec, row,
            pl.BlockSpec((d, qk_dim), lambda b, t: (0, 0)),
            pl.BlockSpec((1, qk_dim), lambda b, t: (0, 0)),
            pl.BlockSpec((KV_DIM, d), lambda b, t: (0, 0)),
            pl.BlockSpec((KV_DIM, 1), lambda b, t: (0, 0)),
            pl.BlockSpec((Q_DIM, d), lambda b, t: (0, 0)),
            row,
        ],
        out_specs=tile,
        out_shape=jax.ShapeDtypeStruct(x.shape, F32),
        scratch_shapes=[
            pltpu.VMEM((WINDOW + ts, 2 * N_KV_HEADS * LANES), BF16),
            pltpu.VMEM((KV_DIM, WINDOW + ts), BF16),
            pltpu.VMEM((2, N_HEADS, BLOCK, BLOCK), F32),
            pltpu.VMEM((2, BLOCK, BLOCK), BF16),
        ],
        compiler_params=pltpu.CompilerParams(
            dimension_semantics=("arbitrary", "arbitrary"),
            vmem_limit_bytes=VMEM_LIMIT_BYTES),
        name="swa_attn",
    )(sinks, x, sh, sc, g, norm_w, w_qk, b_qk, w_vt, b_vt, w_o.astype(BF16), b_o.reshape(1, d))


def _conv_kernel(x_ref, sh_ref, sc_ref, g_ref, nw_ref, wpw1_ref, bpw1_ref, wdw_ref, bdw_ref,
                 lng_ref, lnb_ref, wpw2_ref, bpw2_ref, o_ref, u_ref, v_ref, z_ref, *, ts):
    t = pl.program_id(1)
    seg = ts // SUBLANES
    pitch = seg + CONV_PITCH_PAD
    n_slabs = D_MODEL // LANES

    @pl.when(t == 0)
    def _():
        u_ref[:, 0:seg, :] = jnp.zeros((n_slabs, seg, LANES), F32)

    x = x_ref[0]
    h = _modnorm(x, nw_ref[...], sc_ref[0], sh_ref[0])
    slabs_per_chunk = CONV_COLS // LANES

    def gate(c):
        cols = slice(c * CONV_COLS, (c + 1) * CONV_COLS)
        gcols = slice(D_MODEL + c * CONV_COLS, D_MODEL + (c + 1) * CONV_COLS)
        ua = jnp.dot(h, wpw1_ref[:, cols], preferred_element_type=F32) + bpw1_ref[:, cols]
        ug = jnp.dot(h, wpw1_ref[:, gcols], preferred_element_type=F32) + bpw1_ref[:, gcols]
        glu = ua * jax.nn.sigmoid(ug)
        for a in range(SUBLANES):
            for i in range(slabs_per_chunk):
                u_ref[c * slabs_per_chunk + i, (a + 1) * pitch:(a + 1) * pitch + seg, :] = (
                    glu[a * seg:(a + 1) * seg, i * LANES:(i + 1) * LANES])

    def tile(s, i):
        start = pitch + i if i >= 0 else seg + i
        return u_ref.at[s][pl.ds(start, SUBLANES, stride=pitch), :]

    def conv(s):
        cols = slice(s * LANES, (s + 1) * LANES)
        w = [jnp.broadcast_to(wdw_ref[k:k + 1, cols], (SUBLANES, LANES)) for k in range(CONV_WIDTH)]
        h0, h1 = w[0::2], w[1::2]
        off = (CONV_WIDTH - 1) // 2
        bias = jnp.broadcast_to(bdw_ref[:, cols], (SUBLANES, LANES))
        a_part = []
        for m in range(seg // 2 + 1):
            acc = bias
            for i in range(len(h0)):
                n = m + i - off
                if 2 * n < seg:
                    acc = acc + h0[i] * tile(s, 2 * n)
            a_part.append(acc)
        b_part = []
        for m in range(seg // 2):
            acc = h1[0] * tile(s, 2 * (m - off) + 1)
            for i in range(1, len(h1)):
                acc = acc + h1[i] * tile(s, 2 * (m + i - off) + 1)
            b_part.append(acc)
            v_ref.at[s][pl.ds(2 * m, SUBLANES, stride=pitch), :] = a_part[m] + acc
        for n in range(-off, off + 1):
            z = tile(s, 2 * n + 1)
            if 2 * n + 2 < seg:
                z = z + tile(s, 2 * n + 2)
            z_ref[s, (n + off) * SUBLANES:(n + off + 1) * SUBLANES, :] = z
        hs = [h0[i] + h1[i] if i < len(h1) else h0[i] for i in range(len(h0))]
        bias2 = bias + bias
        for m in range(seg // 2):
            acc = bias2
            for i in range(len(hs)):
                acc = acc + hs[i] * z_ref[s, (m + i) * SUBLANES:(m + i + 1) * SUBLANES, :]
            v_ref.at[s][pl.ds(2 * m + 1, SUBLANES, stride=pitch), :] = acc - a_part[m + 1] - b_part[m]
        u_ref[s, 0:seg, :] = u_ref[s, SUBLANES * pitch:SUBLANES * pitch + seg, :]

    n_chunks = D_MODEL // CONV_COLS
    gate(0)
    for c in range(n_chunks):
        if c + 1 < n_chunks:
            gate(c + 1)
        for i in range(slabs_per_chunk):
            conv(c * slabs_per_chunk + i)

    z = jnp.concatenate(
        [jnp.concatenate([v_ref[s, a * pitch:a * pitch + seg, :] for a in range(SUBLANES)], axis=0)
         for s in range(n_slabs)], axis=1)
    mu = jnp.mean(z, axis=-1, keepdims=True)
    zc = z - mu
    var = jnp.mean(zc * zc, axis=-1, keepdims=True)
    z = zc * lax.rsqrt(var + EPS) * lng_ref[...] + lnb_ref[...]
    z = (z * jax.nn.sigmoid(z)).astype(BF16)
    y = jnp.dot(z, wpw2_ref[...], preferred_element_type=F32) + bpw2_ref[...]
    o_ref[0] = x + g_ref[0] * y


def _conv_layer(x, sh, sc, g, norm_w, w_pw1, b_pw1, w_dw, b_dw, ln_g, ln_b, w_pw2, b_pw2):
    batch, seq, d = x.shape
    ts = CONV_TOKENS
    pitch = ts // SUBLANES + CONV_PITCH_PAD
    vec = pl.BlockSpec((1, 1, d), lambda b, t: (b, 0, 0))
    row = pl.BlockSpec((1, d), lambda b, t: (0, 0))
    tile = pl.BlockSpec((1, ts, d), lambda b, t: (b, t, 0))
    return pl.pallas_call(
        functools.partial(_conv_kernel, ts=ts),
        grid=(batch, seq // ts),
        in_specs=[
            tile, vec, vec, vec, row,
            pl.BlockSpec((d, 2 * d), lambda b, t: (0, 0)),
            pl.BlockSpec((1, 2 * d), lambda b, t: (0, 0)),
            pl.BlockSpec((CONV_WIDTH, d), lambda b, t: (0, 0)),
            row, row, row,
            pl.BlockSpec((d, d), lambda b, t: (0, 0)),
            row,
        ],
        out_specs=tile,
        out_shape=jax.ShapeDtypeStruct(x.shape, F32),
        scratch_shapes=[
            pltpu.VMEM((d // LANES, (SUBLANES + 1) * pitch, LANES), F32),
            pltpu.VMEM((d // LANES, SUBLANES * pitch, LANES), F32),
            pltpu.VMEM((d // LANES, CONV_WIDTH * SUBLANES, LANES), F32),
        ],
        compiler_params=pltpu.CompilerParams(
            dimension_semantics=("arbitrary", "arbitrary"),
            vmem_limit_bytes=VMEM_LIMIT_BYTES),
        name="conformer_conv",
    )(x, sh, sc, g, norm_w, w_pw1, b_pw1, w_dw, b_dw, ln_g, ln_b, w_pw2, b_pw2)


def kernel(x, c, w_mod, b_mod, norm_mix, norm_mlp, w_qkv, b_qkv, w_o, b_o, sinks, w_pw1, b_pw1,
           w_dw, b_dw, conv_ln_g, conv_ln_b, w_pw2, b_pw2, w_up, w_down, final_norm):
    batch, _, d = x.shape
    mod = _modulation(c, w_mod, b_mod)
    mod = mod.reshape(DEPTH, batch, N_MOD, 1, d)
    row = lambda a: a.reshape(1, -1)
    final_w = row(final_norm)
    w_up = w_up.astype(BF16)
    w_down = w_down.astype(BF16)
    for i in range(DEPTH):
        sh1, sc1, g1, sh2, sc2, g2 = (mod[i, :, m] for m in range(N_MOD))
        j = i // 2
        if i % 2 == 0:
            x = _attn_layer(x, sh1, sc1, g1, row(norm_mix[i]), w_qkv[j], b_qkv[j], w_o[j], b_o[j],
                            sinks[j])
        else:
            x = _conv_layer(x, sh1, sc1, g1, row(norm_mix[i]), w_pw1[j].astype(BF16), row(b_pw1[j]),
                            w_dw[j], row(b_dw[j]), row(conv_ln_g[j]), row(conv_ln_b[j]),
                            w_pw2[j].astype(BF16), row(b_pw2[j]))
        x = _mlp_layer(x, sh2, sc2, g2, row(norm_mlp[i]), w_up, w_down, final_w,
                       layer=i, final=(i == DEPTH - 1))
    return x
```

```python
import functools

import jax
import jax.numpy as jnp
from jax import lax
from jax.experimental import pallas as pl
from jax.experimental.pallas import tpu as pltpu

D_MODEL = 1024
DEPTH = 4
N_HEADS = 16
N_KV_HEADS = 2
HEAD_DIM = 64
GROUP = N_HEADS // N_KV_HEADS
WINDOW = 128
BLOCK = 128
SPAN = BLOCK + WINDOW
Q_DIM = N_HEADS * HEAD_DIM
KV_DIM = N_KV_HEADS * HEAD_DIM
CONV_WIDTH = 31
D_FF = 4 * D_MODEL
N_MOD = 6
EPS = 1e-6

LANES = 128
SUBLANES = 8
CONV_PITCH_PAD = 4
VMEM_LIMIT_BYTES = 56 * 1024 * 1024

MLP_TOKENS = 1024
MLP_SUB = 512
MLP_FF_CHUNK = 1024
ATTN_TOKENS = 1024
ATTN_SUB = 256
ATTN_COLS = 256
CONV_TOKENS = 512
CONV_SUB = 256
CONV_COLS = 256
CONV_FIR_DEPTH = 2
CONV_STASH_TILES = 80

F32 = jnp.float32
BF16 = jnp.bfloat16


def _modnorm(x, norm_w, scale, shift):
    r = lax.rsqrt(jnp.mean(x * x, axis=-1, keepdims=True) + EPS)
    return ((x * r) * (norm_w * (1.0 + scale)) + shift).astype(BF16)


def _mod_kernel(c_ref, w_ref, b_ref, o_ref):
    c = c_ref[...]
    cs = c * jax.nn.sigmoid(c)
    o_ref[0] = jnp.dot(cs, w_ref[0], preferred_element_type=F32) + b_ref[0]


def _modulation(c, w_mod, b_mod):
    depth, d, n = w_mod.shape
    batch = c.shape[0]
    nb = D_MODEL
    return pl.pallas_call(
        _mod_kernel,
        grid=(depth, n // nb),
        in_specs=[
            pl.BlockSpec((batch, d), lambda i, j: (0, 0)),
            pl.BlockSpec((1, d, nb), lambda i, j: (i, 0, j)),
            pl.BlockSpec((1, 1, nb), lambda i, j: (i, 0, j)),
        ],
        out_specs=pl.BlockSpec((1, batch, nb), lambda i, j: (i, 0, j)),
        out_shape=jax.ShapeDtypeStruct((depth, batch, n), F32),
        compiler_params=pltpu.CompilerParams(
            dimension_semantics=("arbitrary", "arbitrary"),
            vmem_limit_bytes=VMEM_LIMIT_BYTES),
        name="adaln_mod",
    )(c, w_mod, b_mod.reshape(depth, 1, n))


def _mlp_kernel(x_ref, sh_ref, sc_ref, g_ref, nw_ref, wup_ref, wdn_ref, fn_ref, o_ref, *, tm, final):
    n_sub = tm // MLP_SUB
    n_chunks = D_FF // MLP_FF_CHUNK
    xs, hs = {}, {}

    def norm(sub):
        xs[sub] = x_ref[0, sub * MLP_SUB:(sub + 1) * MLP_SUB, :]
        hs[sub] = _modnorm(xs[sub], nw_ref[...], sc_ref[0], sh_ref[0])

    def up(sub, j):
        cols = slice(j * MLP_FF_CHUNK, (j + 1) * MLP_FF_CHUNK)
        return jnp.dot(hs[sub], wup_ref[0, :, cols], preferred_element_type=F32)

    def ffn(sub):
        x = xs.pop(sub)
        acc = jnp.zeros(x.shape, F32)
        u_next = up(sub, 0)
        for j in range(n_chunks):
            u = u_next
            if j + 1 < n_chunks:
                u_next = up(sub, j + 1)
            rows = slice(j * MLP_FF_CHUNK, (j + 1) * MLP_FF_CHUNK)
            act = jnp.square(jnp.maximum(u, 0.0)).astype(BF16)
            acc = acc + jnp.dot(act, wdn_ref[0, rows, :], preferred_element_type=F32)
        y = x + g_ref[0] * acc
        if final:
            r = lax.rsqrt(jnp.mean(y * y, axis=-1, keepdims=True) + EPS)
            y = y * r * fn_ref[...]
        o_ref[0, sub * MLP_SUB:(sub + 1) * MLP_SUB, :] = y

    norm(0)
    for sub in range(n_sub):
        if sub + 1 < n_sub:
            norm(sub + 1)
        ffn(sub)


def _mlp_layer(x, sh, sc, g, norm_w, w_up, w_dn, final_w, *, layer, final):
    batch, seq, d = x.shape
    tm = MLP_TOKENS
    vec = pl.BlockSpec((1, 1, d), lambda b, t: (b, 0, 0))
    row = pl.BlockSpec((1, d), lambda b, t: (0, 0))
    tile = pl.BlockSpec((1, tm, d), lambda b, t: (b, t, 0))
    return pl.pallas_call(
        functools.partial(_mlp_kernel, tm=tm, final=final),
        grid=(batch, seq // tm),
        in_specs=[
            tile, vec, vec, vec, row,
            pl.BlockSpec((1, d, D_FF), lambda b, t: (layer, 0, 0), pipeline_mode=pl.Buffered(1)),
            pl.BlockSpec((1, D_FF, d), lambda b, t: (layer, 0, 0), pipeline_mode=pl.Buffered(1)),
            row,
        ],
        out_specs=tile,
        out_shape=jax.ShapeDtypeStruct(x.shape, F32),
        compiler_params=pltpu.CompilerParams(
            dimension_semantics=("arbitrary", "arbitrary"),
            vmem_limit_bytes=VMEM_LIMIT_BYTES),
        name="mlp_final" if final else "mlp",
    )(x, sh, sc, g, norm_w, w_up, w_dn, final_w)


def _alibi_slope(head):
    return 2.0 ** (-8.0 * (head + 1) / N_HEADS)


def _attn_kernel(sinks_ref, x_ref, sh_ref, sc_ref, g_ref, nw_ref, wqk_ref, bqk_ref, wvt_ref, bvt_ref,
                 wo_ref, bo_ref, o_ref, k_ref, vt_ref, bias_ref, mask_ref, *, ts):
    b = pl.program_id(0)
    t = pl.program_id(1)
    si = lax.broadcasted_iota(jnp.int32, (BLOCK, BLOCK), 0)
    qi = lax.broadcasted_iota(jnp.int32, (BLOCK, BLOCK), 1)
    cur = si <= qi

    @pl.when((b == 0) & (t == 0))
    def _():
        dist = jnp.where(cur, qi - si, qi - si + WINDOW).astype(F32)
        for hd in range(N_HEADS):
            bias = -_alibi_slope(hd) * dist
            bias_ref[0, hd] = bias
            bias_ref[1, hd] = jnp.where(cur, bias, -jnp.inf)
        mask_ref[0] = jnp.where(cur, 0.0, 1.0).astype(BF16)
        mask_ref[1] = jnp.where(cur, 1.0, 0.0).astype(BF16)

    @pl.when(t == 0)
    def _():
        k_ref[0:WINDOW, :] = jnp.zeros((WINDOW, k_ref.shape[1]), BF16)
        vt_ref[:, 0:WINDOW] = jnp.zeros((KV_DIM, WINDOW), BF16)

    low = lax.broadcasted_iota(jnp.int32, (ATTN_SUB, LANES), 1) < HEAD_DIM
    nt = (((1,), (1,)), ((), ()))
    n_sub = ts // ATTN_SUB
    n_chunks = Q_DIM // ATTN_COLS
    groups = [(sub, qb, kvh) for sub in range(n_sub) for qb in range(ATTN_SUB // BLOCK)
              for kvh in range(N_KV_HEADS)]
    per_sub = len(groups) // n_sub
    xs, hs, q_parts, scores, head_out, attn = {}, {}, {}, {}, {}, {}

    def norm(sub):
        xs[sub] = x_ref[0, sub * ATTN_SUB:(sub + 1) * ATTN_SUB, :]
        hs[sub] = _modnorm(xs[sub], nw_ref[...], sc_ref[0], sh_ref[0])
        q_parts[sub] = [None] * n_chunks

    def project(sub, c):
        s0 = sub * ATTN_SUB
        last = c == n_chunks - 1
        cols = slice(c * ATTN_COLS, Q_DIM + KV_DIM if last else (c + 1) * ATTN_COLS)
        qk = jnp.dot(hs[sub], wqk_ref[:, cols], preferred_element_type=F32) + bqk_ref[:, cols]
        q_parts[sub][c] = (qk[:, :ATTN_COLS] * (HEAD_DIM ** -0.5)).astype(BF16)
        if last:
            kf = qk[:, ATTN_COLS:]
            swapped = pltpu.roll(kf, HEAD_DIM, 1)
            copies = [jnp.where(low, kf, 0.0), jnp.where(low, 0.0, swapped),
                      jnp.where(low, swapped, 0.0), jnp.where(low, 0.0, kf)]
            for j, a in enumerate(copies):
                k_ref[WINDOW + s0:WINDOW + s0 + ATTN_SUB, j * LANES:(j + 1) * LANES] = a.astype(BF16)
            vt = lax.dot_general(wvt_ref[...], hs[sub], nt, preferred_element_type=F32) + bvt_ref[...]
            vt_ref[:, WINDOW + s0:WINDOW + s0 + ATTN_SUB] = vt.astype(BF16)

    def score(g):
        sub, qb, kvh = groups[g]
        r0 = sub * ATTN_SUB + qb * BLOCK
        rows = slice(qb * BLOCK, (qb + 1) * BLOCK)
        pairs = [q_parts[sub][(kvh * GROUP // 2 + i) * LANES // ATTN_COLS]
                 [rows, ((kvh * GROUP // 2 + i) * LANES) % ATTN_COLS:
                        ((kvh * GROUP // 2 + i) * LANES) % ATTN_COLS + LANES] for i in range(GROUP // 2)]
        q4 = jnp.concatenate(pairs, axis=0)
        scores[g] = [lax.dot_general(k_ref[r0:r0 + SPAN, (2 * kvh + half) * LANES:(2 * kvh + half + 1) * LANES],
                                     q4, nt, preferred_element_type=F32) for half in range(2)]

    def attend(g):
        sub, qb, kvh = groups[g]
        r0 = sub * ATTN_SUB + qb * BLOCK
        table = jnp.where(t == 0, 1, 0) if r0 == 0 else 0
        probs = [None] * GROUP
        for half in range(2):
            st = scores[g][half]
            for i in range(GROUP // 2):
                hd = kvh * GROUP + 2 * i + half
                sp = st[:, i * BLOCK:(i + 1) * BLOCK]
                s = jnp.where(cur, sp[WINDOW:], sp[:WINDOW]) + bias_ref[table, hd]
                sink = sinks_ref[hd]
                m = jnp.maximum(jnp.max(s, axis=0, keepdims=True), sink)
                p = jnp.exp(s - m)
                denom = jnp.sum(p, axis=0, keepdims=True) + jnp.exp(sink - m)
                p = (p * (1.0 / denom)).astype(BF16)
                probs[2 * i + half] = jnp.concatenate([p * mask_ref[0], p * mask_ref[1]], axis=0)
        del scores[g]
        vspan = vt_ref[kvh * HEAD_DIM:(kvh + 1) * HEAD_DIM, r0:r0 + SPAN]
        ot = jnp.dot(vspan, jnp.concatenate(probs, axis=1), preferred_element_type=F32)
        for j in range(GROUP):
            head_out[(sub, qb, kvh * GROUP + j)] = ot[:, j * BLOCK:(j + 1) * BLOCK].astype(BF16)

    def gather(sub):
        attn_t = jnp.concatenate(
            [jnp.concatenate([head_out.pop((sub, qb, hd)) for hd in range(N_HEADS)], axis=0)
             for qb in range(ATTN_SUB // BLOCK)], axis=1)
        attn[sub] = attn_t.T

    def output(sub, c):
        cols = slice(c * ATTN_COLS, (c + 1) * ATTN_COLS)
        y = jnp.dot(attn[sub], wo_ref[:, cols], preferred_element_type=F32) + bo_ref[:, cols]
        o_ref[0, sub * ATTN_SUB:(sub + 1) * ATTN_SUB, cols] = xs[sub][:, cols] + g_ref[0][:, cols] * y

    norm(0)
    for c in range(n_chunks):
        project(0, c)
    score(0)
    for g in range(len(groups)):
        sub, j = divmod(g, per_sub)
        if sub + 1 < n_sub:
            if j == 0:
                norm(sub + 1)
            project(sub + 1, j)
        if sub >= 1:
            output(sub - 1, j)
        if g + 1 < len(groups):
            score(g + 1)
        attend(g)
        if j == per_sub - 1:
            gather(sub)
    for c in range(n_chunks):
        output(n_sub - 1, c)

    k_ref[0:WINDOW, :] = k_ref[ts:ts + WINDOW, :]
    vt_ref[:, 0:WINDOW] = vt_ref[:, ts:ts + WINDOW]


def _attn_layer(x, sh, sc, g, norm_w, w_qkv, b_qkv, w_o, b_o, sinks):
    batch, seq, d = x.shape
    ts = ATTN_TOKENS
    qk_dim = Q_DIM + KV_DIM
    w_qk = w_qkv[:, :qk_dim].astype(BF16)
    w_vt = w_qkv[:, qk_dim:].T.astype(BF16)
    b_qk = b_qkv[:qk_dim].reshape(1, qk_dim)
    b_vt = b_qkv[qk_dim:].reshape(KV_DIM, 1)
    vec = pl.BlockSpec((1, 1, d), lambda b, t: (b, 0, 0))
    row = pl.BlockSpec((1, d), lambda b, t: (0, 0))
    tile = pl.BlockSpec((1, ts, d), lambda b, t: (b, t, 0))
    return pl.pallas_call(
        functools.partial(_attn_kernel, ts=ts),
        grid=(batch, seq // ts),
        in_specs=[
            pl.BlockSpec(memory_space=pltpu.SMEM),
            tile, vec, vec, vec, row,
            pl.BlockSpec((d, qk_dim), lambda b, t: (0, 0)),
            pl.BlockSpec((1, qk_dim), lambda b, t: (0, 0)),
            pl.BlockSpec((KV_DIM, d), lambda b, t: (0, 0)),
            pl.BlockSpec((KV_DIM, 1), lambda b, t: (0, 0)),
            pl.BlockSpec((Q_DIM, d), lambda b, t: (0, 0)),
            row,
        ],
        out_specs=tile,
        out_shape=jax.ShapeDtypeStruct(x.shape, F32),
        scratch_shapes=[
            pltpu.VMEM((WINDOW + ts, 2 * N_KV_HEADS * LANES), BF16),
            pltpu.VMEM((KV_DIM, WINDOW + ts), BF16),
            pltpu.VMEM((2, N_HEADS, BLOCK, BLOCK), F32),
            pltpu.VMEM((2, BLOCK, BLOCK), BF16),
        ],
        compiler_params=pltpu.CompilerParams(
            dimension_semantics=("arbitrary", "arbitrary"),
            vmem_limit_bytes=VMEM_LIMIT_BYTES),
        name="swa_attn",
    )(sinks, x, sh, sc, g, norm_w, w_qk, b_qk, w_vt, b_vt, w_o.astype(BF16), b_o.reshape(1, d))


def _fast_fir(taps, get, count, init, stash, depth):
    if depth == 0 or count < 4:
        w = [f() for f in taps]
        out = []
        for m in range(count):
            acc = init
            for i, wi in enumerate(w):
                x = get(m + i)
                if x is not None:
                    acc = wi * x if acc is None else acc + wi * x
            out.append(acc)
        return out
    if count % 2:
        return (_fast_fir(taps, get, count - 1, init, stash, depth)
                + _fast_fir(taps, lambda n: get(count - 1 + n), 1, init, stash, 0))
    half = count // 2
    t0, t1 = taps[0::2], taps[1::2]
    g0 = lambda n: get(2 * n)
    g1 = lambda n: get(2 * n + 1)
    a = _fast_fir(t0, g0, half + 1, init, stash, depth - 1)
    b = _fast_fir(t1, g1, half, None, stash, depth - 1)
    mixed = {}
    for n in range(half + len(t0) - 1):
        x1, x0 = g1(n), g0(n + 1)
        if x1 is not None or x0 is not None:
            mixed[n] = stash(x1 if x0 is None else x0 if x1 is None else x1 + x0)
    ts = [(lambda i=i: t0[i]() + t1[i]()) if i < len(t1) else t0[i] for i in range(len(t0))]
    p = _fast_fir(ts, lambda n: mixed[n]() if n in mixed else None, half,
                  None if init is None else init + init, stash, depth - 1)
    out = []
    for q in range(half):
        out += [a[q] + b[q], p[q] - a[q + 1] - b[q]]
    return out


def _conv_kernel(x_ref, sh_ref, sc_ref, g_ref, nw_ref, wpw1_ref, bpw1_ref, wdw_ref, bdw_ref,
                 lng_ref, lnb_ref, wpw2_ref, bpw2_ref, o_ref, u_ref, v_ref, z_ref, *, ts):
    t = pl.program_id(1)
    seg = CONV_SUB // SUBLANES
    pitch = seg + CONV_PITCH_PAD
    n_slabs = D_MODEL // LANES
    n_chunks = D_MODEL // CONV_COLS

    @pl.when(t == 0)
    def _():
        u_ref[:, 0:seg, :] = jnp.zeros((n_slabs, seg, LANES), F32)

    def sub_tile(sub):
        rows = slice(sub * CONV_SUB, (sub + 1) * CONV_SUB)
        x = x_ref[0, rows, :]
        h = _modnorm(x, nw_ref[...], sc_ref[0], sh_ref[0])
        slabs_per_chunk = CONV_COLS // LANES

        def gate(c):
            cols = slice(c * CONV_COLS, (c + 1) * CONV_COLS)
            gcols = slice(D_MODEL + c * CONV_COLS, D_MODEL + (c + 1) * CONV_COLS)
            ua = jnp.dot(h, wpw1_ref[:, cols], preferred_element_type=F32) + bpw1_ref[:, cols]
            ug = jnp.dot(h, wpw1_ref[:, gcols], preferred_element_type=F32) + bpw1_ref[:, gcols]
            glu = ua * jax.nn.sigmoid(ug)
            for a in range(SUBLANES):
                for i in range(slabs_per_chunk):
                    u_ref[c * slabs_per_chunk + i, (a + 1) * pitch:(a + 1) * pitch + seg, :] = (
                        glu[a * seg:(a + 1) * seg, i * LANES:(i + 1) * LANES])

        def tile(s, i):
            start = pitch + i if i >= 0 else seg + i
            return u_ref.at[s][pl.ds(start, SUBLANES, stride=pitch), :]

        def conv(s):
            cols = slice(s * LANES, (s + 1) * LANES)
            taps = [functools.partial(lambda k: jnp.broadcast_to(wdw_ref[k:k + 1, cols], (SUBLANES, LANES)), k)
                    for k in range(CONV_WIDTH)]
            slots = iter(range(z_ref.shape[1] // SUBLANES))

            def stash(value):
                rows = pl.ds(next(slots) * SUBLANES, SUBLANES)
                z_ref[s, rows, :] = value
                return lambda: z_ref[s, rows, :]

            def get(n):
                i = n - (CONV_WIDTH - 1)
                return tile(s, i) if i < seg else None

            bias = jnp.broadcast_to(bdw_ref[:, cols], (SUBLANES, LANES))
            for j, y in enumerate(_fast_fir(taps, get, seg, bias, stash, CONV_FIR_DEPTH)):
                v_ref.at[s][pl.ds(j, SUBLANES, stride=pitch), :] = y
            u_ref[s, 0:seg, :] = u_ref[s, SUBLANES * pitch:SUBLANES * pitch + seg, :]

        gate(0)
        for c in range(n_chunks):
            if c + 1 < n_chunks:
                gate(c + 1)
            for i in range(slabs_per_chunk):
                conv(c * slabs_per_chunk + i)

        z = jnp.concatenate(
            [jnp.concatenate([v_ref[s, a * pitch:a * pitch + seg, :] for a in range(SUBLANES)], axis=0)
             for s in range(n_slabs)], axis=1)
        mu = jnp.mean(z, axis=-1, keepdims=True)
        zc = z - mu
        var = jnp.mean(zc * zc, axis=-1, keepdims=True)
        z = zc * lax.rsqrt(var + EPS) * lng_ref[...] + lnb_ref[...]
        z = (z * jax.nn.sigmoid(z)).astype(BF16)
        y = jnp.dot(z, wpw2_ref[...], preferred_element_type=F32) + bpw2_ref[...]
        o_ref[0, rows, :] = x + g_ref[0] * y

    for sub in range(ts // CONV_SUB):
        sub_tile(sub)


def _conv_layer(x, sh, sc, g, norm_w, w_pw1, b_pw1, w_dw, b_dw, ln_g, ln_b, w_pw2, b_pw2):
    batch, seq, d = x.shape
    ts = CONV_TOKENS
    pitch = CONV_SUB // SUBLANES + CONV_PITCH_PAD
    vec = pl.BlockSpec((1, 1, d), lambda b, t: (b, 0, 0))
    row = pl.BlockSpec((1, d), lambda b, t: (0, 0))
    tile = pl.BlockSpec((1, ts, d), lambda b, t: (b, t, 0))
    return pl.pallas_call(
        functools.partial(_conv_kernel, ts=ts),
        grid=(batch, seq // ts),
        in_specs=[
            tile, vec, vec, vec, row,
            pl.BlockSpec((d, 2 * d), lambda b, t: (0, 0)),
            pl.BlockSpec((1, 2 * d), lambda b, t: (0, 0)),
            pl.BlockSpec((CONV_WIDTH, d), lambda b, t: (0, 0)),
            row, row, row,
            pl.BlockSpec((d, d), lambda b, t: (0, 0)),
            row,
        ],
        out_specs=tile,
        out_shape=jax.ShapeDtypeStruct(x.shape, F32),
        scratch_shapes=[
            pltpu.VMEM((d // LANES, (SUBLANES + 1) * pitch, LANES), F32),
            pltpu.VMEM((d // LANES, SUBLANES * pitch, LANES), F32),
            pltpu.VMEM((d // LANES, CONV_STASH_TILES * SUBLANES, LANES), F32),
        ],
        compiler_params=pltpu.CompilerParams(
            dimension_semantics=("arbitrary", "arbitrary"),
            vmem_limit_bytes=VMEM_LIMIT_BYTES),
        name="conformer_conv",
    )(x, sh, sc, g, norm_w, w_pw1, b_pw1, w_dw, b_dw, ln_g, ln_b, w_pw2, b_pw2)


def kernel(x, c, w_mod, b_mod, norm_mix, norm_mlp, w_qkv, b_qkv, w_o, b_o, sinks, w_pw1, b_pw1,
           w_dw, b_dw, conv_ln_g, conv_ln_b, w_pw2, b_pw2, w_up, w_down, final_norm):
    batch, _, d = x.shape
    mod = _modulation(c, w_mod, b_mod)
    mod = mod.reshape(DEPTH, batch, N_MOD, 1, d)
    row = lambda a: a.reshape(1, -1)
    final_w = row(final_norm)
    w_up = w_up.astype(BF16)
    w_down = w_down.astype(BF16)
    for i in range(DEPTH):
        sh1, sc1, g1, sh2, sc2, g2 = (mod[i, :, m] for m in range(N_MOD))
        j = i // 2
        if i % 2 == 0:
            x = _attn_layer(x, sh1, sc1, g1, row(norm_mix[i]), w_qkv[j], b_qkv[j], w_o[j], b_o[j],
                            sinks[j])
        else:
            x = _conv_layer(x, sh1, sc1, g1, row(norm_mix[i]), w_pw1[j].astype(BF16), row(b_pw1[j]),
                            w_dw[j], row(b_dw[j]), row(conv_ln_g[j]), row(conv_ln_b[j]),
                            w_pw2[j].astype(BF16), row(b_pw2[j]))
        x = _mlp_layer(x, sh2, sc2, g2, row(norm_mlp[i]), w_up, w_down, final_w,
                       layer=i, final=(i == DEPTH - 1))
    return x
```

```python
import functools

import jax
import jax.numpy as jnp
from jax import lax
from jax.experimental import pallas as pl
from jax.experimental.pallas import tpu as pltpu

D_MODEL = 1024
DEPTH = 4
N_HEADS = 16
N_KV_HEADS = 2
HEAD_DIM = 64
GROUP = N_HEADS // N_KV_HEADS
WINDOW = 128
BLOCK = 128
SPAN = BLOCK + WINDOW
Q_DIM = N_HEADS * HEAD_DIM
KV_DIM = N_KV_HEADS * HEAD_DIM
CONV_WIDTH = 31
D_FF = 4 * D_MODEL
N_MOD = 6
EPS = 1e-6

LANES = 128
SUBLANES = 8
CONV_PITCH_PAD = 4
VMEM_LIMIT_BYTES = 56 * 1024 * 1024

MLP_TOKENS = 1024
MLP_SUB = 512
MLP_FF_CHUNK = 1024
ATTN_TOKENS = 1024
ATTN_SUB = 256
ATTN_COLS = 256
CONV_TOKENS = 512
CONV_SUB = 256
CONV_COLS = 256
CONV_FIR_DEPTH = 2
CONV_STASH_TILES = 80

F32 = jnp.float32
BF16 = jnp.bfloat16


def _modnorm(x, norm_w, scale, shift):
    r = lax.rsqrt(jnp.mean(x * x, axis=-1, keepdims=True) + EPS)
    return ((x * r) * (norm_w * (1.0 + scale)) + shift).astype(BF16)


def _mod_kernel(c_ref, w_ref, b_ref, o_ref):
    c = c_ref[...]
    cs = c * jax.nn.sigmoid(c)
    o_ref[0] = jnp.dot(cs, w_ref[0], preferred_element_type=F32) + b_ref[0]


def _modulation(c, w_mod, b_mod):
    depth, d, n = w_mod.shape
    batch = c.shape[0]
    nb = D_MODEL
    return pl.pallas_call(
        _mod_kernel,
        grid=(depth, n // nb),
        in_specs=[
            pl.BlockSpec((batch, d), lambda i, j: (0, 0)),
            pl.BlockSpec((1, d, nb), lambda i, j: (i, 0, j)),
            pl.BlockSpec((1, 1, nb), lambda i, j: (i, 0, j)),
        ],
        out_specs=pl.BlockSpec((1, batch, nb), lambda i, j: (i, 0, j)),
        out_shape=jax.ShapeDtypeStruct((depth, batch, n), F32),
        compiler_params=pltpu.CompilerParams(
            dimension_semantics=("arbitrary", "arbitrary"),
            vmem_limit_bytes=VMEM_LIMIT_BYTES),
        name="adaln_mod",
    )(c, w_mod, b_mod.reshape(depth, 1, n))


def _mlp_kernel(x_ref, sh_ref, sc_ref, g_ref, nw_ref, wup_ref, wdn_ref, fn_ref, o_ref, *, tm, final):
    n_sub = tm // MLP_SUB
    n_chunks = D_FF // MLP_FF_CHUNK
    xs, hs = {}, {}

    def norm(sub):
        xs[sub] = x_ref[0, sub * MLP_SUB:(sub + 1) * MLP_SUB, :]
        hs[sub] = _modnorm(xs[sub], nw_ref[...], sc_ref[0], sh_ref[0])

    def up(sub, j):
        cols = slice(j * MLP_FF_CHUNK, (j + 1) * MLP_FF_CHUNK)
        return jnp.dot(hs[sub], wup_ref[0, :, cols], preferred_element_type=F32)

    def ffn(sub):
        x = xs.pop(sub)
        acc = jnp.zeros(x.shape, F32)
        u_next = up(sub, 0)
        for j in range(n_chunks):
            u = u_next
            if j + 1 < n_chunks:
                u_next = up(sub, j + 1)
            rows = slice(j * MLP_FF_CHUNK, (j + 1) * MLP_FF_CHUNK)
            act = jnp.square(jnp.maximum(u, 0.0)).astype(BF16)
            acc = acc + jnp.dot(act, wdn_ref[0, rows, :], preferred_element_type=F32)
        y = x + g_ref[0] * acc
        if final:
            r = lax.rsqrt(jnp.mean(y * y, axis=-1, keepdims=True) + EPS)
            y = y * r * fn_ref[...]
        o_ref[0, sub * MLP_SUB:(sub + 1) * MLP_SUB, :] = y

    norm(0)
    for sub in range(n_sub):
        if sub + 1 < n_sub:
            norm(sub + 1)
        ffn(sub)


def _mlp_layer(x, sh, sc, g, norm_w, w_up, w_dn, final_w, *, layer, final):
    batch, seq, d = x.shape
    tm = MLP_TOKENS
    vec = pl.BlockSpec((1, 1, d), lambda b, t: (b, 0, 0))
    row = pl.BlockSpec((1, d), lambda b, t: (0, 0))
    tile = pl.BlockSpec((1, tm, d), lambda b, t: (b, t, 0))
    return pl.pallas_call(
        functools.partial(_mlp_kernel, tm=tm, final=final),
        grid=(batch, seq // tm),
        in_specs=[
            tile, vec, vec, vec, row,
            pl.BlockSpec((1, d, D_FF), lambda b, t: (layer, 0, 0), pipeline_mode=pl.Buffered(1)),
            pl.BlockSpec((1, D_FF, d), lambda b, t: (layer, 0, 0), pipeline_mode=pl.Buffered(1)),
            row,
        ],
        out_specs=tile,
        out_shape=jax.ShapeDtypeStruct(x.shape, F32),
        compiler_params=pltpu.CompilerParams(
            dimension_semantics=("arbitrary", "arbitrary"),
            vmem_limit_bytes=VMEM_LIMIT_BYTES),
        name="mlp_final" if final else "mlp",
    )(x, sh, sc, g, norm_w, w_up, w_dn, final_w)


def _alibi_slope(head):
    return 2.0 ** (-8.0 * (head + 1) / N_HEADS)


def _attn_kernel(sinks_ref, x_ref, sh_ref, sc_ref, g_ref, nw_ref, wqk_ref, bqk_ref, wvt_ref, bvt_ref,
                 wo_ref, bo_ref, o_ref, k_ref, vt_ref, bias_ref, mask_ref, *, ts):
    b = pl.program_id(0)
    t = pl.program_id(1)
    si = lax.broadcasted_iota(jnp.int32, (BLOCK, BLOCK), 0)
    qi = lax.broadcasted_iota(jnp.int32, (BLOCK, BLOCK), 1)
    cur = si <= qi

    @pl.when((b == 0) & (t == 0))
    def _():
        dist = jnp.where(cur, qi - si, qi - si + WINDOW).astype(F32)
        for hd in range(N_HEADS):
            bias = -_alibi_slope(hd) * dist
            bias_ref[0, hd] = bias
            bias_ref[1, hd] = jnp.where(cur, bias, -jnp.inf)
        mask_ref[0] = jnp.where(cur, 0.0, 1.0).astype(BF16)
        mask_ref[1] = jnp.where(cur, 1.0, 0.0).astype(BF16)

    @pl.when(t == 0)
    def _():
        k_ref[0:WINDOW, :] = jnp.zeros((WINDOW, k_ref.shape[1]), BF16)
        vt_ref[:, 0:WINDOW] = jnp.zeros((KV_DIM, WINDOW), BF16)

    low = lax.broadcasted_iota(jnp.int32, (ATTN_SUB, LANES), 1) < HEAD_DIM
    nt = (((1,), (1,)), ((), ()))
    n_sub = ts // ATTN_SUB
    n_chunks = Q_DIM // ATTN_COLS
    groups = [(sub, qb, kvh) for sub in range(n_sub) for qb in range(ATTN_SUB // BLOCK)
              for kvh in range(N_KV_HEADS)]
    per_sub = len(groups) // n_sub
    xs, hs, q_parts, scores, head_out, attn = {}, {}, {}, {}, {}, {}

    def norm(sub):
        xs[sub] = x_ref[0, sub * ATTN_SUB:(sub + 1) * ATTN_SUB, :]
        hs[sub] = _modnorm(xs[sub], nw_ref[...], sc_ref[0], sh_ref[0])
        q_parts[sub] = [None] * n_chunks

    def project(sub, c):
        s0 = sub * ATTN_SUB
        last = c == n_chunks - 1
        cols = slice(c * ATTN_COLS, Q_DIM + KV_DIM if last else (c + 1) * ATTN_COLS)
        qk = jnp.dot(hs[sub], wqk_ref[:, cols], preferred_element_type=F32) + bqk_ref[:, cols]
        q_parts[sub][c] = (qk[:, :ATTN_COLS] * (HEAD_DIM ** -0.5)).astype(BF16)
        if last:
            kf = qk[:, ATTN_COLS:]
            swapped = pltpu.roll(kf, HEAD_DIM, 1)
            copies = [jnp.where(low, kf, 0.0), jnp.where(low, 0.0, swapped),
                      jnp.where(low, swapped, 0.0), jnp.where(low, 0.0, kf)]
            for j, a in enumerate(copies):
                k_ref[WINDOW + s0:WINDOW + s0 + ATTN_SUB, j * LANES:(j + 1) * LANES] = a.astype(BF16)
            vt = lax.dot_general(wvt_ref[...], hs[sub], nt, preferred_element_type=F32) + bvt_ref[...]
            vt_ref[:, WINDOW + s0:WINDOW + s0 + ATTN_SUB] = vt.astype(BF16)

    def score(g):
        sub, qb, kvh = groups[g]
        r0 = sub * ATTN_SUB + qb * BLOCK
        rows = slice(qb * BLOCK, (qb + 1) * BLOCK)
        pairs = [q_parts[sub][(kvh * GROUP // 2 + i) * LANES // ATTN_COLS]
                 [rows, ((kvh * GROUP // 2 + i) * LANES) % ATTN_COLS:
                        ((kvh * GROUP // 2 + i) * LANES) % ATTN_COLS + LANES] for i in range(GROUP // 2)]
        q4 = jnp.concatenate(pairs, axis=0)
        scores[g] = [lax.dot_general(k_ref[r0:r0 + SPAN, (2 * kvh + half) * LANES:(2 * kvh + half + 1) * LANES],
                                     q4, nt, preferred_element_type=F32) for half in range(2)]

    def attend(g):
        sub, qb, kvh = groups[g]
        r0 = sub * ATTN_SUB + qb * BLOCK
        table = jnp.where(t == 0, 1, 0) if r0 == 0 else 0
        probs = [None] * GROUP
        for half in range(2):
            st = scores[g][half]
            for i in range(GROUP // 2):
                hd = kvh * GROUP + 2 * i + half
                sp = st[:, i * BLOCK:(i + 1) * BLOCK]
                s = jnp.where(cur, sp[WINDOW:], sp[:WINDOW]) + bias_ref[table, hd]
                sink = sinks_ref[hd]
                m = jnp.maximum(jnp.max(s, axis=0, keepdims=True), sink)
                p = jnp.exp(s - m)
                denom = jnp.sum(p, axis=0, keepdims=True) + jnp.exp(sink - m)
                p = (p * (1.0 / denom)).astype(BF16)
                probs[2 * i + half] = jnp.concatenate([p * mask_ref[0], p * mask_ref[1]], axis=0)
        del scores[g]
        vspan = vt_ref[kvh * HEAD_DIM:(kvh + 1) * HEAD_DIM, r0:r0 + SPAN]
        ot = jnp.dot(vspan, jnp.concatenate(probs, axis=1), preferred_element_type=F32)
        for j in range(GROUP):
            head_out[(sub, qb, kvh * GROUP + j)] = ot[:, j * BLOCK:(j + 1) * BLOCK].astype(BF16)

    def gather(sub):
        attn_t = jnp.concatenate(
            [jnp.concatenate([head_out.pop((sub, qb, hd)) for hd in range(N_HEADS)], axis=0)
             for qb in range(ATTN_SUB // BLOCK)], axis=1)
        attn[sub] = attn_t.T

    def output(sub, c):
        cols = slice(c * ATTN_COLS, (c + 1) * ATTN_COLS)
        y = jnp.dot(attn[sub], wo_ref[:, cols], preferred_element_type=F32) + bo_ref[:, cols]
        o_ref[0, sub * ATTN_SUB:(sub + 1) * ATTN_SUB, cols] = xs[sub][:, cols] + g_ref[0][:, cols] * y

    norm(0)
    for c in range(n_chunks):
        project(0, c)
    score(0)
    for g in range(len(groups)):
        sub, j = divmod(g, per_sub)
        if sub + 1 < n_sub:
            if j == 0:
                norm(sub + 1)
            project(sub + 1, j)
        if sub >= 1:
            output(sub - 1, j)
        if g + 1 < len(groups):
            score(g + 1)
        attend(g)
        if j == per_sub - 1:
            gather(sub)
    for c in range(n_chunks):
        output(n_sub - 1, c)

    k_ref[0:WINDOW, :] = k_ref[ts:ts + WINDOW, :]
    vt_ref[:, 0:WINDOW] = vt_ref[:, ts:ts + WINDOW]


def _attn_layer(x, sh, sc, g, norm_w, w_qkv, b_qkv, w_o, b_o, sinks):
    batch, seq, d = x.shape
    ts = ATTN_TOKENS
    qk_dim = Q_DIM + KV_DIM
    w_qk = w_qkv[:, :qk_dim].astype(BF16)
    w_vt = w_qkv[:, qk_dim:].T.astype(BF16)
    b_qk = b_qkv[:qk_dim].reshape(1, qk_dim)
    b_vt = b_qkv[qk_dim:].reshape(KV_DIM, 1)
    vec = pl.BlockSpec((1, 1, d), lambda b, t: (b, 0, 0))
    row = pl.BlockSpec((1, d), lambda b, t: (0, 0))
    tile = pl.BlockSpec((1, ts, d), lambda b, t: (b, t, 0))
    return pl.pallas_call(
        functools.partial(_attn_kernel, ts=ts),
        grid=(batch, seq // ts),
        in_specs=[
            pl.BlockSpec(memory_space=pltpu.SMEM),
            tile, vec, vec, vec, row,
            pl.BlockSpec((d, qk_dim), lambda b, t: (0, 0)),
            pl.BlockSpec((1, qk_dim), lambda b, t: (0, 0)),
            pl.BlockSpec((KV_DIM, d), lambda b, t: (0, 0)),
            pl.BlockSpec((KV_DIM, 1), lambda b, t: (0, 0)),
            pl.BlockSpec((Q_DIM, d), lambda b, t: (0, 0)),
            row,
        ],
        out_specs=tile,
        out_shape=jax.ShapeDtypeStruct(x.shape, F32),
        scratch_shapes=[
            pltpu.VMEM((WINDOW + ts, 2 * N_KV_HEADS * LANES), BF16),
            pltpu.VMEM((KV_DIM, WINDOW + ts), BF16),
            pltpu.VMEM((2, N_HEADS, BLOCK, BLOCK), F32),
            pltpu.VMEM((2, BLOCK, BLOCK), BF16),
        ],
        compiler_params=pltpu.CompilerParams(
            dimension_semantics=("arbitrary", "arbitrary"),
            vmem_limit_bytes=VMEM_LIMIT_BYTES),
        name="swa_attn",
    )(sinks, x, sh, sc, g, norm_w, w_qk, b_qk, w_vt, b_vt, w_o.astype(BF16), b_o.reshape(1, d))


def _fast_fir(taps, get, count, init, stash, depth):
    if depth == 0 or count < 4:
        w = [f() for f in taps]
        out = []
        for m in range(count):
            acc = init
            for i, wi in enumerate(w):
                x = get(m + i)
                if x is not None:
                    acc = wi * x if acc is None else acc + wi * x
            out.append(acc)
        return out
    if count % 2:
        return (_fast_fir(taps, get, count - 1, init, stash, depth)
                + _fast_fir(taps, lambda n: get(count - 1 + n), 1, init, stash, 0))
    half = count // 2
    t0, t1 = taps[0::2], taps[1::2]
    g0 = lambda n: get(2 * n)
    g1 = lambda n: get(2 * n + 1)
    a = _fast_fir(t0, g0, half + 1, init, stash, depth - 1)
    b = _fast_fir(t1, g1, half, None, stash, depth - 1)
    mixed = {}
    for n in range(half + len(t0) - 1):
        x1, x0 = g1(n), g0(n + 1)
        if x1 is not None or x0 is not None:
            mixed[n] = stash(x1 if x0 is None else x0 if x1 is None else x1 + x0)
    ts = [(lambda i=i: t0[i]() + t1[i]()) if i < len(t1) else t0[i] for i in range(len(t0))]
    p = _fast_fir(ts, lambda n: mixed[n]() if n in mixed else None, half,
                  None if init is None else init + init, stash, depth - 1)
    out = []
    for q in range(half):
        out += [a[q] + b[q], p[q] - a[q + 1] - b[q]]
    return out


def _conv_kernel(x_ref, sh_ref, sc_ref, g_ref, nw_ref, wpw1_ref, bpw1_ref, wdw_ref, bdw_ref,
                 lng_ref, lnb_ref, wpw2_ref, bpw2_ref, o_ref, u_ref, v_ref, z_ref, h_ref, *, ts):
    t = pl.program_id(1)
    seg = CONV_SUB // SUBLANES
    pitch = seg + CONV_PITCH_PAD
    n_slabs = D_MODEL // LANES
    n_chunks = D_MODEL // CONV_COLS

    @pl.when(t == 0)
    def _():
        u_ref[:, 0:seg, :] = jnp.zeros((n_slabs, seg, LANES), F32)

    slabs_per_chunk = CONV_COLS // LANES
    n_units = (ts // CONV_SUB) * n_chunks

    def norm(sub):
        rows = slice(sub * CONV_SUB, (sub + 1) * CONV_SUB)
        h_ref[...] = _modnorm(x_ref[0, rows, :], nw_ref[...], sc_ref[0], sh_ref[0])

    def gate(c):
        cols = slice(c * CONV_COLS, (c + 1) * CONV_COLS)
        gcols = slice(D_MODEL + c * CONV_COLS, D_MODEL + (c + 1) * CONV_COLS)
        ua = jnp.dot(h_ref[...], wpw1_ref[:, cols], preferred_element_type=F32) + bpw1_ref[:, cols]
        ug = jnp.dot(h_ref[...], wpw1_ref[:, gcols], preferred_element_type=F32) + bpw1_ref[:, gcols]
        glu = ua * jax.nn.sigmoid(ug)
        for a in range(SUBLANES):
            for i in range(slabs_per_chunk):
                u_ref[c * slabs_per_chunk + i, (a + 1) * pitch:(a + 1) * pitch + seg, :] = (
                    glu[a * seg:(a + 1) * seg, i * LANES:(i + 1) * LANES])

    def tile(s, i):
        start = pitch + i if i >= 0 else seg + i
        return u_ref.at[s][pl.ds(start, SUBLANES, stride=pitch), :]

    def conv(s):
        cols = slice(s * LANES, (s + 1) * LANES)
        taps = [functools.partial(lambda k: jnp.broadcast_to(wdw_ref[k:k + 1, cols], (SUBLANES, LANES)), k)
                for k in range(CONV_WIDTH)]
        slots = iter(range(z_ref.shape[1] // SUBLANES))

        def stash(value):
            rows = pl.ds(next(slots) * SUBLANES, SUBLANES)
            z_ref[s, rows, :] = value
            return lambda: z_ref[s, rows, :]

        def get(n):
            i = n - (CONV_WIDTH - 1)
            return tile(s, i) if i < seg else None

        bias = jnp.broadcast_to(bdw_ref[:, cols], (SUBLANES, LANES))
        for j, y in enumerate(_fast_fir(taps, get, seg, bias, stash, CONV_FIR_DEPTH)):
            v_ref.at[s][pl.ds(j, SUBLANES, stride=pitch), :] = y
        u_ref[s, 0:seg, :] = u_ref[s, SUBLANES * pitch:SUBLANES * pitch + seg, :]
        return y

    def after(value):
        word = pltpu.bitcast(value, jnp.uint32)
        zero = lax.shift_right_logical(lax.shift_right_logical(word, jnp.uint32(16)), jnp.uint32(16))
        zero = jnp.concatenate([zero, zero], axis=0).astype(F32).astype(BF16)
        h_ref[0:2 * SUBLANES, 0:LANES] = h_ref[0:2 * SUBLANES, 0:LANES] + zero

    def finish(sub):
        rows = slice(sub * CONV_SUB, (sub + 1) * CONV_SUB)
        z = jnp.concatenate(
            [jnp.concatenate([v_ref[s, a * pitch:a * pitch + seg, :] for a in range(SUBLANES)], axis=0)
             for s in range(n_slabs)], axis=1)
        mu = jnp.mean(z, axis=-1, keepdims=True)
        zc = z - mu
        var = jnp.mean(zc * zc, axis=-1, keepdims=True)
        z = zc * lax.rsqrt(var + EPS) * lng_ref[...] + lnb_ref[...]
        z = (z * jax.nn.sigmoid(z)).astype(BF16)
        y = jnp.dot(z, wpw2_ref[...], preferred_element_type=F32) + bpw2_ref[...]
        o_ref[0, rows, :] = x_ref[0, rows, :] + g_ref[0] * y

    norm(0)
    gate(0)
    gate(1)
    for k in range(n_units):
        sub, c = divmod(k, n_chunks)
        for i in range(slabs_per_chunk):
            last = conv(c * slabs_per_chunk + i)
        if k + 2 < n_units:
            sub2, c2 = divmod(k + 2, n_chunks)
            if c2 == 0:
                norm(sub2)
            after(last)
            gate(c2)
        if c == n_chunks - 1:
            finish(sub)


def _conv_layer(x, sh, sc, g, norm_w, w_pw1, b_pw1, w_dw, b_dw, ln_g, ln_b, w_pw2, b_pw2):
    batch, seq, d = x.shape
    ts = CONV_TOKENS
    pitch = CONV_SUB // SUBLANES + CONV_PITCH_PAD
    vec = pl.BlockSpec((1, 1, d), lambda b, t: (b, 0, 0))
    row = pl.BlockSpec((1, d), lambda b, t: (0, 0))
    tile = pl.BlockSpec((1, ts, d), lambda b, t: (b, t, 0))
    return pl.pallas_call(
        functools.partial(_conv_kernel, ts=ts),
        grid=(batch, seq // ts),
        in_specs=[
            tile, vec, vec, vec, row,
            pl.BlockSpec((d, 2 * d), lambda b, t: (0, 0)),
            pl.BlockSpec((1, 2 * d), lambda b, t: (0, 0)),
            pl.BlockSpec((CONV_WIDTH, d), lambda b, t: (0, 0)),
            row, row, row,
            pl.BlockSpec((d, d), lambda b, t: (0, 0)),
            row,
        ],
        out_specs=tile,
        out_shape=jax.ShapeDtypeStruct(x.shape, F32),
        scratch_shapes=[
            pltpu.VMEM((d // LANES, (SUBLANES + 1) * pitch, LANES), F32),
            pltpu.VMEM((d // LANES, SUBLANES * pitch, LANES), F32),
            pltpu.VMEM((d // LANES, CONV_STASH_TILES * SUBLANES, LANES), F32),
            pltpu.VMEM((CONV_SUB, d), BF16),
        ],
        compiler_params=pltpu.CompilerParams(
            dimension_semantics=("arbitrary", "arbitrary"),
            vmem_limit_bytes=VMEM_LIMIT_BYTES),
        name="conformer_conv",
    )(x, sh, sc, g, norm_w, w_pw1, b_pw1, w_dw, b_dw, ln_g, ln_b, w_pw2, b_pw2)


def kernel(x, c, w_mod, b_mod, norm_mix, norm_mlp, w_qkv, b_qkv, w_o, b_o, sinks, w_pw1, b_pw1,
           w_dw, b_dw, conv_ln_g, conv_ln_b, w_pw2, b_pw2, w_up, w_down, final_norm):
    batch, _, d = x.shape
    mod = _modulation(c, w_mod, b_mod)
    mod = mod.reshape(DEPTH, batch, N_MOD, 1, d)
    row = lambda a: a.reshape(1, -1)
    final_w = row(final_norm)
    w_up = w_up.astype(BF16)
    w_down = w_down.astype(BF16)
    for i in range(DEPTH):
        sh1, sc1, g1, sh2, sc2, g2 = (mod[i, :, m] for m in range(N_MOD))
        j = i // 2
        if i % 2 == 0:
            x = _attn_layer(x, sh1, sc1, g1, row(norm_mix[i]), w_qkv[j], b_qkv[j], w_o[j], b_o[j],
                            sinks[j])
        else:
            x = _conv_layer(x, sh1, sc1, g1, row(norm_mix[i]), w_pw1[j].astype(BF16), row(b_pw1[j]),
                            w_dw[j], row(b_dw[j]), row(conv_ln_g[j]), row(conv_ln_b[j]),
                            w_pw2[j].astype(BF16), row(b_pw2[j]))
        x = _mlp_layer(x, sh2, sc2, g2, row(norm_mlp[i]), w_up, w_down, final_w,
                       layer=i, final=(i == DEPTH - 1))
    return x
```
